```python
import math
import jax, jax.numpy as jnp
from jax import lax
import numpy as np

D_MODEL = 4096
BATCH = 4
SEQ = 4096
DEPTH = 4

GRID_W = 64
CTX_LEN = 256

W_GMLP = D_MODEL // 4
W_SSM = D_MODEL // 2
W_ATTN = D_MODEL // 4
D_MIX = W_GMLP + W_SSM + W_ATTN

GMLP_CHUNK = 128
GMLP_GROUPS = 8
GMLP_GROUP_W = W_GMLP // GMLP_GROUPS

SSM_HEAD_DIM = 64
SSM_HEADS = W_SSM // SSM_HEAD_DIM
SSM_GROUPS = 8
SSM_STATE = 128
SSM_CHUNK = 128
SSM_CONV = 3
SSM_CONV_CH = W_SSM + 2 * SSM_GROUPS * SSM_STATE

ATTN_HEADS = 8
ATTN_QK_DIM = W_ATTN // ATTN_HEADS // 2
ATTN_V_DIM = 2 * ATTN_QK_DIM
ATTN_BLOCK = 128
ROPE_BASE = 10000.0

FFN_HIDDEN = ((8 * D_MODEL // 3 + 255) // 256) * 256
ADA_RANK = 512
EPS = 1e-6

PROJ_SIZES = (W_GMLP, W_GMLP, W_SSM, SSM_CONV_CH, 2 * SSM_HEADS, W_ATTN, W_ATTN, W_ATTN)
N_IN = sum(PROJ_SIZES)

kernel_name = "hybrid_gmlp_ssd_diffattn_dit"


def rms_norm(x, g, eps=EPS):
    xf = x.astype(jnp.float32)
    y = xf * lax.rsqrt(jnp.mean(xf * xf, axis=-1, keepdims=True) + eps)
    return (y * g.astype(jnp.float32)).astype(x.dtype)


def modulate(x, g, shift, scale):
    return rms_norm(x, g) * (1.0 + scale) + shift


def adaln(cond, w_down, w_up, bias):
    h = (jax.nn.silu(cond) @ w_down) @ w_up + bias
    return jnp.split(h, 6, axis=-1)


def split_proj(p):
    return jnp.split(p, np.cumsum(PROJ_SIZES)[:-1].tolist(), axis=-1)


def swiglu(h, w1, w2):
    g, u = jnp.split(h @ w1, 2, axis=-1)
    return (jax.nn.silu(g) * u) @ w2


def chunk_gmlp(u, v, g_v, w_s, b_s):
    b, L, _ = u.shape
    n = L // GMLP_CHUNK
    u = jax.nn.gelu(u)
    v = rms_norm(jax.nn.gelu(v), g_v).reshape(b, n, GMLP_CHUNK, GMLP_GROUPS, GMLP_GROUP_W)
    s = jnp.einsum('gij,bnjgc->bnigc', w_s, v) + b_s.T[None, None, :, :, None]
    return u * s.reshape(b, L, W_GMLP)


def centred_dwconv(x, w, bias):
    k = w.shape[0]
    y = lax.conv_general_dilated(
        x, w[:, None, :].astype(x.dtype), window_strides=(1,),
        padding=[(k // 2, k // 2)], dimension_numbers=('NWC', 'WIO', 'NWC'),
        feature_group_count=x.shape[-1])
    return y + bias


def ssd_scan(x, dt, A, B, C, h0):
    b, L, H, P = x.shape
    G, N = B.shape[2], B.shape[3]
    E = H // G
    nc = L // SSM_CHUNK
    Q = SSM_CHUNK
    xf = x.astype(jnp.float32).reshape(b, nc, Q, G, E, P)
    dt = dt.reshape(b, nc, Q, G, E)
    Bf = B.astype(jnp.float32).reshape(b, nc, Q, G, N)
    Cf = C.astype(jnp.float32).reshape(b, nc, Q, G, N)
    a_cum = jnp.cumsum(dt * A.reshape(G, E), axis=2)
    xdt = xf * dt[..., None]
    a_t = jnp.moveaxis(a_cum, 2, -1)
    seg = a_t[..., :, None] - a_t[..., None, :]
    tril = jnp.tril(jnp.ones((Q, Q), dtype=bool))
    Lmat = jnp.exp(jnp.where(tril, seg, -jnp.inf))
    CB = jnp.einsum('bciGn,bcjGn->bcGij', Cf, Bf)
    y_diag = jnp.einsum('bcGij,bcGEij,bcjGEp->bciGEp', CB, Lmat, xdt)
    decay_to_end = jnp.exp(a_cum[:, :, -1:] - a_cum)
    chunk_states = jnp.einsum('bcjGn,bcjGE,bcjGEp->bcGEpn', Bf, decay_to_end, xdt)
    chunk_decay = jnp.exp(a_cum[:, :, -1])

    def step(h, inp):
        s, d = inp
        return h * d[..., None, None] + s, h

    h_init = h0.astype(jnp.float32).reshape(b, G, E, P, N)
    h_fin, h_prev = lax.scan(step, h_init, (jnp.moveaxis(chunk_states, 1, 0),
                                            jnp.moveaxis(chunk_decay, 1, 0)))
    h_prev = jnp.moveaxis(h_prev, 0, 1)
    y_off = jnp.einsum('bciGn,bcGEpn,bciGE->bciGEp', Cf, h_prev, jnp.exp(a_cum))
    y = (y_diag + y_off).reshape(b, L, H, P).astype(x.dtype)
    return y, h_fin.reshape(b, H, P, N)


def ssm_branch(z, xbc, dt_raw, h0_f, h0_b, conv_w, conv_b, dt_bias, a_log, d_skip, g_norm):
    b, L, _ = z.shape
    xbc = jax.nn.silu(centred_dwconv(xbc, conv_w, conv_b))
    xs, bm, cm = jnp.split(xbc, [W_SSM, W_SSM + SSM_GROUPS * SSM_STATE], axis=-1)
    xs = xs.reshape(b, L, SSM_HEADS, SSM_HEAD_DIM)
    bm = bm.reshape(b, L, SSM_GROUPS, SSM_STATE)
    cm = cm.reshape(b, L, SSM_GROUPS, SSM_STATE)
    dt = jax.nn.softplus((dt_raw.reshape(b, L, 2, SSM_HEADS) + dt_bias).astype(jnp.float32))
    A = -jnp.exp(a_log.astype(jnp.float32))
    y_f, h_f = ssd_scan(xs, dt[:, :, 0], A[0], bm, cm, h0_f)
    y_b, h_b = ssd_scan(jnp.flip(xs, 1), jnp.flip(dt[:, :, 1], 1), A[1],
                        jnp.flip(bm, 1), jnp.flip(cm, 1), h0_b)
    y = y_f + jnp.flip(y_b, 1) + d_skip[:, None] * xs
    y = y.reshape(b, L, W_SSM) * jax.nn.silu(z)
    y = rms_norm(y.reshape(b, L, SSM_GROUPS, -1), g_norm.reshape(SSM_GROUPS, -1))
    return y.reshape(b, L, W_SSM), h_f, h_b


def axial_rope_tables(L):
    rows_n = L // GRID_W
    row = jnp.repeat(jnp.arange(rows_n), GRID_W)
    col = jnp.tile(jnp.arange(GRID_W), rows_n)
    n_freq = ATTN_QK_DIM // 4
    inv = ROPE_BASE ** (-jnp.arange(n_freq, dtype=jnp.float32) / n_freq)
    ang = jnp.concatenate([row[:, None] * inv, col[:, None] * inv], axis=-1)
    return jnp.cos(ang), jnp.sin(ang)


def apply_rope(t, cos, sin):
    shp = t.shape
    tp = t.astype(jnp.float32).reshape(shp[:-1] + (shp[-1] // 2, 2))
    cs = cos[None, :, None, None, :]
    sn = sin[None, :, None, None, :]
    t1, t2 = tp[..., 0], tp[..., 1]
    out = jnp.stack([t1 * cs - t2 * sn, t1 * sn + t2 * cs], axis=-1)
    return out.reshape(shp).astype(t.dtype)


def qk_head(t, g, cos=None, sin=None):
    t = rms_norm(t.reshape(t.shape[:-1] + (ATTN_HEADS, 2, ATTN_QK_DIM)), g)
    if cos is not None:
        t = apply_rope(t, cos, sin)
    return t


def diff_attend(q, k, v, lam):
    s = jnp.einsum('bqhmd,bkhmd->bhmqk', q, k).astype(jnp.float32) * (ATTN_QK_DIM ** -0.5)
    p = jax.nn.softmax(s, axis=-1)
    w = (p[:, :, 0] - lam * p[:, :, 1]).astype(v.dtype)
    return jnp.einsum('bhqk,bkhe->bqhe', w, v)


def blocked_diff_attend(q, k, v, lam):
    b, L, H, two, d = q.shape
    nb = L // ATTN_BLOCK
    qb = jnp.moveaxis(q.reshape(b, nb, ATTN_BLOCK, H, two, d), 1, 0)
    o = lax.map(lambda qi: diff_attend(qi, k, v, lam), qb)
    return jnp.moveaxis(o, 0, 1).reshape(b, L, H, v.shape[-1])


def attn_post(o, g, lam_init):
    b, L = o.shape[0], o.shape[1]
    return (rms_norm(o, g) * (1.0 - lam_init)).reshape(b, L, W_ATTN)


def setup_inputs(seed: int = 0) -> dict:
    key = jax.random.key(seed)
    ks = jax.random.split(key, 32)
    f32 = jnp.float32

    def nrm(k, shape, scale):
        return jax.random.normal(k, shape, f32) * scale

    def gain(k, shape):
        return 1.0 + 0.02 * jax.random.normal(k, shape, f32)

    dt0 = jnp.exp(jax.random.uniform(ks[14], (DEPTH, 2, SSM_HEADS), f32,
                                     math.log(1e-3), math.log(1e-1)))
    return {
        "x": nrm(ks[0], (BATCH, SEQ, D_MODEL), 1.0),
        "c": nrm(ks[1], (BATCH, D_MODEL), 1.0),
        "ctx": nrm(ks[2], (BATCH, CTX_LEN, D_MODEL), 1.0),
        "c_ctx": nrm(ks[3], (D_MODEL,), 1.0),
        "ada_down": nrm(ks[4], (DEPTH, D_MODEL, ADA_RANK), D_MODEL ** -0.5),
        "ada_up": nrm(ks[5], (DEPTH, ADA_RANK, 6 * D_MODEL), 0.5 * ADA_RANK ** -0.5),
        "ada_bias": nrm(ks[6], (DEPTH, 6 * D_MODEL), 0.01),
        "norm1_g": gain(ks[7], (DEPTH, D_MODEL)),
        "w_in": nrm(ks[8], (DEPTH, D_MODEL, N_IN), D_MODEL ** -0.5),
        "gm_norm_g": gain(ks[9], (DEPTH, W_GMLP)),
        "gm_w": nrm(ks[10], (DEPTH, GMLP_GROUPS, GMLP_CHUNK, GMLP_CHUNK), GMLP_CHUNK ** -0.5),
        "gm_b": gain(ks[11], (DEPTH, GMLP_GROUPS, GMLP_CHUNK)),
        "conv_w": nrm(ks[12], (DEPTH, SSM_CONV, SSM_CONV_CH), SSM_CONV ** -0.5),
        "conv_b": nrm(ks[13], (DEPTH, SSM_CONV_CH), 0.02),
        "dt_bias": dt0 + jnp.log(-jnp.expm1(-dt0)),
        "a_log": jnp.log(jax.random.uniform(ks[15], (DEPTH, 2, SSM_HEADS), f32, 1.0, 16.0)),
        "d_skip": 1.0 + 0.1 * jax.random.normal(ks[16], (DEPTH, SSM_HEADS), f32),
        "ssm_norm_g": gain(ks[17], (DEPTH, W_SSM)),
        "q_norm_g": gain(ks[18], (DEPTH, ATTN_QK_DIM)),
        "k_norm_g": gain(ks[19], (DEPTH, ATTN_QK_DIM)),
        "lam_q": nrm(ks[20], (DEPTH, 2, ATTN_QK_DIM), 0.1),
        "lam_k": nrm(ks[21], (DEPTH, 2, ATTN_QK_DIM), 0.1),
        "subln_g": gain(ks[22], (DEPTH, ATTN_V_DIM)),
        "w_out": nrm(ks[23], (DEPTH, D_MIX, D_MODEL), D_MIX ** -0.5),
        "norm2_g": gain(ks[24], (DEPTH, D_MODEL)),
        "w_ffn_in": nrm(ks[25], (DEPTH, D_MODEL, 2 * FFN_HIDDEN), D_MODEL ** -0.5),
        "w_ffn_out": nrm(ks[26], (DEPTH, FFN_HIDDEN, D_MODEL), FFN_HIDDEN ** -0.5),
    }


def reference(x, c, ctx, c_ctx, ada_down, ada_up, ada_bias, norm1_g, w_in, gm_norm_g,
              gm_w, gm_b, conv_w, conv_b, dt_bias, a_log, d_skip, ssm_norm_g,
              q_norm_g, k_norm_g, lam_q, lam_k, subln_g, w_out, norm2_g,
              w_ffn_in, w_ffn_out):
    b, L, _ = x.shape
    lc = ctx.shape[1]
    cos, sin = axial_rope_tables(L)
    h0 = jnp.zeros((b, SSM_HEADS, SSM_HEAD_DIM, SSM_STATE), jnp.float32)
    s = ctx
    for l in range(DEPTH):
        last = l == DEPTH - 1
        sh1, sc1, ga1, sh2, sc2, ga2 = [m[:, None, :] for m in
                                        adaln(c, ada_down[l], ada_up[l], ada_bias[l])]
        csh1, csc1, cga1, csh2, csc2, cga2 = adaln(c_ctx, ada_down[l], ada_up[l], ada_bias[l])

        hl = modulate(x, norm1_g[l], sh1, sc1)
        hc = modulate(s, norm1_g[l], csh1, csc1)
        gu_l, gv_l, z_l, xbc_l, dt_l, q_l, k_l, av_l = split_proj(hl @ w_in[l])
        gu_c, gv_c, z_c, xbc_c, dt_c, q_c, k_c, av_c = split_proj(hc @ w_in[l])

        ssm_p = (conv_w[l], conv_b[l], dt_bias[l], a_log[l], d_skip[l], ssm_norm_g[l])
        y_c, hf_c, hb_c = ssm_branch(z_c, xbc_c, dt_c, h0, h0, *ssm_p)
        y_l, _, _ = ssm_branch(z_l, xbc_l, dt_l, hf_c, hb_c, *ssm_p)

        lam_init = 0.8 - 0.6 * math.exp(-0.3 * l)
        lq = lam_q[l].astype(jnp.float32)
        lk = lam_k[l].astype(jnp.float32)
        lam = jnp.exp(jnp.sum(lq[0] * lk[0])) - jnp.exp(jnp.sum(lq[1] * lk[1])) + lam_init
        kc_h = qk_head(k_c, k_norm_g[l])
        vc_h = av_c.reshape(b, lc, ATTN_HEADS, ATTN_V_DIM)
        kl_h = qk_head(k_l, k_norm_g[l], cos, sin)
        ql_h = qk_head(q_l, q_norm_g[l], cos, sin)
        vl_h = av_l.reshape(b, L, ATTN_HEADS, ATTN_V_DIM)
        keys = jnp.concatenate([kc_h, kl_h], axis=1)
        vals = jnp.concatenate([vc_h, vl_h], axis=1)
        o_l = attn_post(blocked_diff_attend(ql_h, keys, vals, lam), subln_g[l], lam_init)

        a_l = chunk_gmlp(gu_l, gv_l, gm_norm_g[l], gm_w[l], gm_b[l])

        mix_l = jnp.concatenate([a_l, y_l, o_l], axis=-1) @ w_out[l]
        x = x + ga1 * mix_l
        x = x + ga2 * swiglu(modulate(x, norm2_g[l], sh2, sc2), w_ffn_in[l], w_ffn_out[l])

        if not last:
            qc_h = qk_head(q_c, q_norm_g[l])
            o_c = attn_post(diff_attend(qc_h, kc_h, vc_h, lam), subln_g[l], lam_init)
            a_c = chunk_gmlp(gu_c, gv_c, gm_norm_g[l], gm_w[l], gm_b[l])
            mix_c = jnp.concatenate([a_c, y_c, o_c], axis=-1) @ w_out[l]
            s = s + cga1 * mix_c
            s = s + cga2 * swiglu(modulate(s, norm2_g[l], csh2, csc2), w_ffn_in[l], w_ffn_out[l])
    return x
```

```python
import functools
import math

import jax
import jax.numpy as jnp
from jax import lax
from jax.experimental import pallas as pl
from jax.experimental.pallas import tpu as pltpu

F32 = jnp.float32
BF16 = jnp.bfloat16

GRID_W = 64
CHUNK = 128
GMLP_GROUPS = 8
SSM_HEAD_DIM = 64
SSM_GROUPS = 8
SSM_STATE = 128
SSM_HEADS_PER_GROUP = 4
ATTN_HEADS = 8
ATTN_QK_DIM = 64
ROPE_BASE = 10000.0
EPS = 1e-6
ADA_ROWS = 8

LANES = 128
VMEM_CAP = 56 * 1024 * 1024


def _tile(n, pref, mult=8):
    if n <= pref:
        return n
    t = (pref // mult) * mult
    while t > mult and n % t:
        t -= mult
    assert n % t == 0, (n, pref, mult)
    return t


def _cparams(sem, vmem_bytes):
    return pltpu.CompilerParams(dimension_semantics=sem,
                                vmem_limit_bytes=int(min(max(vmem_bytes, 16 << 20), VMEM_CAP)))


def _dot(a, b):
    return jnp.dot(a, b, preferred_element_type=F32)


def _split_bf16(x, terms):
    parts = []
    for _ in range(terms):
        p = x.astype(BF16)
        parts.append(p)
        x = x - p.astype(F32)
    return parts


def _dot_x3(a, b):
    ah, al = _split_bf16(a, 2)
    bh, bl = _split_bf16(b, 2)
    return _dot(ah, bh) + (_dot(ah, bl) + _dot(al, bh))


def _silu(x):
    return x * jax.nn.sigmoid(x)


def _gelu_tanh(x):
    return 0.5 * x * (1.0 + jnp.tanh(math.sqrt(2.0 / math.pi) * (x + 0.044715 * (x * x * x))))


def _softplus(x):
    return jnp.maximum(x, 0.0) + jnp.log1p(jnp.exp(-jnp.abs(x)))


def _mod_row(ref, tiles_per_row, fixed_row, axis=0):
    if tiles_per_row is None:
        return ref[fixed_row:fixed_row + 1, :]
    return ref[pl.ds(pl.program_id(axis) // tiles_per_row, 1), :]


def _adaln_kernel(cc_ref, down_ref, up_ref, bias_ref, o_ref, h_ref):
    @pl.when(pl.program_id(0) == 0)
    def _():
        h_ref[...] = _dot_x3(_silu(cc_ref[...]), down_ref[...])

    o_ref[...] = _dot_x3(h_ref[...], up_ref[...]) + bias_ref[...]


def adaln(cc, down, up, bias):
    d, r = down.shape
    n = up.shape[1]
    tn = _tile(n, 2048, LANES)
    return pl.pallas_call(
        _adaln_kernel,
        grid=(n // tn,),
        in_specs=[pl.BlockSpec((ADA_ROWS, d), lambda j: (0, 0)),
                  pl.BlockSpec((d, r), lambda j: (0, 0)),
                  pl.BlockSpec((r, tn), lambda j: (0, j)),
                  pl.BlockSpec((1, tn), lambda j: (0, j))],
        out_specs=pl.BlockSpec((ADA_ROWS, tn), lambda j: (0, j)),
        out_shape=jax.ShapeDtypeStruct((ADA_ROWS, n), F32),
        scratch_shapes=[pltpu.VMEM((ADA_ROWS, r), F32)],
        compiler_params=_cparams(("arbitrary",), 6 * d * r * 4 + 8 * r * tn * 4),
        name="adaln",
    )(cc, down, up, bias.reshape(1, n))


def _modnorm_kernel(x_ref, g_ref, sh_ref, sc_ref, o_ref, *, tiles_per_row, fixed_row):
    x = x_ref[...]
    y = x * lax.rsqrt(jnp.mean(x * x, axis=-1, keepdims=True) + EPS) * g_ref[...]
    sh = _mod_row(sh_ref, tiles_per_row, fixed_row)
    sc = _mod_row(sc_ref, tiles_per_row, fixed_row)
    o_ref[...] = (y * (1.0 + sc) + sh).astype(o_ref.dtype)


def modnorm(x, g, mods, shift_idx, scale_idx, rows_per_batch, fixed_row):
    m, d = x.shape
    tr = _tile(m if rows_per_batch is None else rows_per_batch, 256)
    tiles_per_row = None if rows_per_batch is None else rows_per_batch // tr
    return pl.pallas_call(
        functools.partial(_modnorm_kernel, tiles_per_row=tiles_per_row, fixed_row=fixed_row),
        grid=(m // tr,),
        in_specs=[pl.BlockSpec((tr, d), lambda i: (i, 0)),
                  pl.BlockSpec((1, d), lambda i: (0, 0)),
                  pl.BlockSpec((ADA_ROWS, d), lambda i: (0, shift_idx)),
                  pl.BlockSpec((ADA_ROWS, d), lambda i: (0, scale_idx))],
        out_specs=pl.BlockSpec((tr, d), lambda i: (i, 0)),
        out_shape=jax.ShapeDtypeStruct((m, d), BF16),
        compiler_params=_cparams(("parallel",), 8 * tr * d * 4),
        name="modnorm",
    )(x, g.reshape(1, d), mods, mods)


def _mm_kernel(a_ref, w_ref, o_ref):
    o_ref[...] = _dot(a_ref[...], w_ref[...]).astype(o_ref.dtype)


def matmul(a, w, out_dtype, tm_pref=1024, tn_pref=1024):
    m, k = a.shape
    n = w.shape[1]
    tm, tn = _tile(m, tm_pref), _tile(n, tn_pref, LANES)
    osz = jnp.dtype(out_dtype).itemsize
    return pl.pallas_call(
        _mm_kernel,
        grid=(m // tm, n // tn),
        in_specs=[pl.BlockSpec((tm, k), lambda i, j: (i, 0)),
                  pl.BlockSpec((k, tn), lambda i, j: (0, j))],
        out_specs=pl.BlockSpec((tm, tn), lambda i, j: (i, j)),
        out_shape=jax.ShapeDtypeStruct((m, n), out_dtype),
        compiler_params=_cparams(("parallel", "parallel"),
                                 2 * (tm * k * 2 + k * tn * 2 + tm * tn * osz) + tm * tn * 4 + (4 << 20)),
        name="matmul",
    )(a, w)


def _mm_resid_kernel(*refs, n_a, k_sizes, tiles_per_row, fixed_row):
    a_refs = refs[:n_a]
    w_ref, r_ref, g_ref, o_ref = refs[n_a:]
    acc = None
    off = 0
    for a_ref, ks in zip(a_refs, k_sizes):
        part = _dot(a_ref[...], w_ref[off:off + ks, :])
        acc = part if acc is None else acc + part
        off += ks
    o_ref[...] = r_ref[...] + _mod_row(g_ref, tiles_per_row, fixed_row) * acc


def matmul_gated_residual(a_list, w, resid, mods, gate_idx, rows_per_batch, fixed_row,
                          tm_pref=1024, tn_pref=512, single_buffer_a=False):
    m, n = resid.shape
    k_sizes = tuple(a.shape[1] for a in a_list)
    k = sum(k_sizes)
    tm = _tile(m if rows_per_batch is None else rows_per_batch, tm_pref)
    tn = _tile(n, tn_pref, LANES)
    tiles_per_row = None if rows_per_batch is None else rows_per_batch // tm
    nblk = n // tn
    a_kw = dict(pipeline_mode=pl.Buffered(1)) if single_buffer_a else {}
    a_specs = [pl.BlockSpec((tm, ks), lambda i, j: (i, 0), **a_kw) for ks in k_sizes]
    a_bufs = 1 if single_buffer_a else 2
    return pl.pallas_call(
        functools.partial(_mm_resid_kernel, n_a=len(a_list), k_sizes=k_sizes,
                          tiles_per_row=tiles_per_row, fixed_row=fixed_row),
        grid=(m // tm, nblk),
        in_specs=a_specs + [pl.BlockSpec((k, tn), lambda i, j: (0, j)),
                            pl.BlockSpec((tm, tn), lambda i, j: (i, j)),
                            pl.BlockSpec((ADA_ROWS, tn), lambda i, j: (0, gate_idx * nblk + j))],
        out_specs=pl.BlockSpec((tm, tn), lambda i, j: (i, j)),
        out_shape=jax.ShapeDtypeStruct((m, n), F32),
        compiler_params=_cparams(("parallel", "parallel"),
                                 a_bufs * tm * k * 2 + 2 * k * tn * 2 + 5 * tm * tn * 4 + (4 << 20)),
        name="matmul_gated_residual",
    )(*a_list, w, resid, mods)


def _mm_swiglu_kernel(a_ref, wg_ref, wu_ref, o_ref):
    a = a_ref[...]
    g = _dot(a, wg_ref[...])
    u = _dot(a, wu_ref[...])
    o_ref[...] = (_silu(g) * u).astype(o_ref.dtype)


def matmul_swiglu(a, w, tm_pref=1024, tn_pref=256):
    m, k = a.shape
    hdim = w.shape[1] // 2
    tm, tn = _tile(m, tm_pref), _tile(hdim, tn_pref, LANES)
    nblk = hdim // tn
    return pl.pallas_call(
        _mm_swiglu_kernel,
        grid=(m // tm, nblk),
        in_specs=[pl.BlockSpec((tm, k), lambda i, j: (i, 0)),
                  pl.BlockSpec((k, tn), lambda i, j: (0, j)),
                  pl.BlockSpec((k, tn), lambda i, j: (0, nblk + j))],
        out_specs=pl.BlockSpec((tm, tn), lambda i, j: (i, j)),
        out_shape=jax.ShapeDtypeStruct((m, hdim), BF16),
        compiler_params=_cparams(("parallel", "parallel"),
                                 2 * (tm * k * 2 + 2 * k * tn * 2 + tm * tn * 2) + 4 * tm * tn * 4 + (4 << 20)),
        name="matmul_swiglu",
    )(a, w, w)


def _gmlp_kernel(gu_ref, gv_ref, gn_ref, w_ref, b_ref, o_ref, *, group_w):
    v = _gelu_tanh(gv_ref[...])
    v = (v * lax.rsqrt(jnp.mean(v * v, axis=-1, keepdims=True) + EPS) * gn_ref[...]).astype(BF16)
    for n in range(gu_ref.shape[0] // CHUNK):
        rows = slice(n * CHUNK, (n + 1) * CHUNK)
        for g in range(GMLP_GROUPS):
            cols = slice(g * group_w, (g + 1) * group_w)
            s = _dot(w_ref[g], v[rows, cols]) + b_ref[:, cols]
            o_ref[rows, cols] = (_gelu_tanh(gu_ref[rows, cols]) * s).astype(o_ref.dtype)


def gmlp(proj, gu_blk, gv_blk, width, gn, w_s, b_full):
    m = proj.shape[0]
    tr = _tile(m, 512, CHUNK)
    return pl.pallas_call(
        functools.partial(_gmlp_kernel, group_w=width // GMLP_GROUPS),
        grid=(m // tr,),
        in_specs=[pl.BlockSpec((tr, width), lambda i: (i, gu_blk)),
                  pl.BlockSpec((tr, width), lambda i: (i, gv_blk)),
                  pl.BlockSpec((1, width), lambda i: (0, 0)),
                  pl.BlockSpec((GMLP_GROUPS, CHUNK, CHUNK), lambda i: (0, 0, 0)),
                  pl.BlockSpec((CHUNK, width), lambda i: (0, 0))],
        out_specs=pl.BlockSpec((tr, width), lambda i: (i, 0)),
        out_shape=jax.ShapeDtypeStruct((m, width), BF16),
        compiler_params=_cparams(("parallel",), 12 * tr * width * 4),
        name="gmlp",
    )(proj, proj, gn.reshape(1, width), w_s, b_full)


def _conv_kernel(x_ref, prev_ref, next_ref, w_ref, b_ref, o_ref, *, tiles_per_seq):
    t = pl.program_id(0) % tiles_per_seq
    x = x_ref[...]
    rows = x.shape[0]
    rid = lax.broadcasted_iota(jnp.int32, x.shape, 0)
    prev_row = jnp.where(t == 0, 0.0, prev_ref[7:8, :])
    next_row = jnp.where(t == tiles_per_seq - 1, 0.0, next_ref[0:1, :])
    x_m1 = jnp.where(rid == 0, prev_row, pltpu.roll(x, 1, 0))
    x_p1 = jnp.where(rid == rows - 1, next_row, pltpu.roll(x, rows - 1, 0))
    y = w_ref[0:1, :] * x_m1 + w_ref[1:2, :] * x + w_ref[2:3, :] * x_p1 + b_ref[...]
    o_ref[...] = _silu(y)


def conv_silu(proj, col_blk, width, seq_len, w, b):
    m = proj.shape[0]
    tr = _tile(seq_len, 256)
    tiles_per_seq = seq_len // tr
    hb = tr // 8
    last = m // 8 - 1
    return pl.pallas_call(
        functools.partial(_conv_kernel, tiles_per_seq=tiles_per_seq),
        grid=(m // tr,),
        in_specs=[pl.BlockSpec((tr, width), lambda i: (i, col_blk)),
                  pl.BlockSpec((8, width), lambda i: (jnp.maximum(i * hb - 1, 0), col_blk)),
                  pl.BlockSpec((8, width), lambda i: (jnp.minimum((i + 1) * hb, last), col_blk)),
                  pl.BlockSpec((3, width), lambda i: (0, 0)),
                  pl.BlockSpec((1, width), lambda i: (0, 0))],
        out_specs=pl.BlockSpec((tr, width), lambda i: (i, 0)),
        out_shape=jax.ShapeDtypeStruct((m, width), F32),
        compiler_params=_cparams(("parallel",), 10 * tr * width * 4),
        name="conv_silu",
    )(proj, proj, proj, w, b.reshape(1, width))


def _bcast_heads(mat, cols):
    rows = mat.shape[0]
    lane = lax.broadcasted_iota(jnp.int32, (rows, LANES), 1)
    col = lambda c: jnp.broadcast_to(mat[:, c:c + 1], (rows, LANES))
    lo = jnp.where(lane < SSM_HEAD_DIM, col(cols[0]), col(cols[1]))
    hi = jnp.where(lane < SSM_HEAD_DIM, col(cols[2]), col(cols[3]))
    return jnp.concatenate([lo, hi], axis=1)


def _ssd_kernel(x_ref, b_ref, c_ref, dt_ref, dtb_ref, alog_ref, h0_ref, y_ref, hfin_ref, h_scr,
                *, reverse, dcol):
    t = pl.program_id(1)
    q = CHUNK
    gw = SSM_HEADS_PER_GROUP * SSM_HEAD_DIM

    @pl.when(t == 0)
    def _():
        h_scr[...] = h0_ref[...]

    dt = _softplus(dt_ref[...] + dtb_ref[...])
    a = dt * (-jnp.exp(alog_ref[...]))
    ii = lax.broadcasted_iota(jnp.int32, (q, q), 0)
    jj = lax.broadcasted_iota(jnp.int32, (q, q), 1)
    tri = (jj >= ii) if reverse else (jj <= ii)
    tri_b = tri.astype(BF16)
    a_parts = _split_bf16(a, 3)
    ac = _dot(tri_b, a_parts[0]) + (_dot(tri_b, a_parts[1]) + _dot(tri_b, a_parts[2]))
    ac_t = ac.T
    edge = 0 if reverse else q - 1
    ac_end = ac[edge:edge + 1, :]
    grow = jnp.exp(ac)
    w_end = dt * jnp.exp(ac_end - ac)
    d_end = jnp.exp(ac_end)
    lane = lax.broadcasted_iota(jnp.int32, (q, LANES), 1)

    for g in range(SSM_GROUPS):
        cols = [dcol + g * SSM_HEADS_PER_GROUP + e for e in range(SSM_HEADS_PER_GROUP)]
        xg = x_ref[:, g * gw:(g + 1) * gw]
        bg = b_ref[:, g * SSM_STATE:(g + 1) * SSM_STATE].astype(BF16)
        cg = c_ref[:, g * SSM_STATE:(g + 1) * SSM_STATE].astype(BF16)
        cb = lax.dot_general(cg, bg, (((1,), (1,)), ((), ())), preferred_element_type=F32)
        xdt = xg * _bcast_heads(dt, cols)
        h_prev = h_scr[g]
        y_parts = []
        for pair in range(SSM_HEADS_PER_GROUP // 2):
            ms = []
            for e in (2 * pair, 2 * pair + 1):
                c = cols[e]
                seg = jnp.broadcast_to(ac[:, c:c + 1], (q, q)) - jnp.broadcast_to(ac_t[c:c + 1, :], (q, q))
                ms.append((cb * jnp.exp(jnp.where(tri, seg, -jnp.inf))).astype(BF16))
            xp = xdt[:, pair * LANES:(pair + 1) * LANES]
            rhs = jnp.concatenate([jnp.where(lane < SSM_HEAD_DIM, xp, 0.0),
                                   jnp.where(lane < SSM_HEAD_DIM, 0.0, xp)], axis=0).astype(BF16)
            y_parts.append(_dot(jnp.concatenate(ms, axis=1), rhs))
        y_diag = jnp.concatenate(y_parts, axis=1)
        y_off = _dot(cg, h_prev.astype(BF16)) * _bcast_heads(grow, cols)
        y_ref[:, g * gw:(g + 1) * gw] = y_diag + y_off
        xdd = (xg * _bcast_heads(w_end, cols)).astype(BF16)
        s_t = lax.dot_general(bg, xdd, (((0,), (0,)), ((), ())), preferred_element_type=F32)
        h_scr[g] = h_prev * _bcast_heads(d_end, cols) + s_t

    @pl.when(t == pl.num_programs(1) - 1)
    def _():
        hfin_ref[...] = h_scr[...]


def ssd_scan(xbc, dt_raw, dt_bias_row, a_log_row, h0, batch, seq_len, d_inner, reverse, dcol):
    nchunks = seq_len // CHUNK
    gn = SSM_GROUPS * SSM_STATE
    gw = SSM_HEADS_PER_GROUP * SSM_HEAD_DIM
    bblk = d_inner // gn
    if reverse:
        row = lambda b, t: b * nchunks + (nchunks - 1 - t)
    else:
        row = lambda b, t: b * nchunks + t
    state_spec = pl.BlockSpec((None, SSM_GROUPS, SSM_STATE, gw), lambda b, t: (b, 0, 0, 0))
    return pl.pallas_call(
        functools.partial(_ssd_kernel, reverse=reverse, dcol=dcol),
        grid=(batch, nchunks),
        in_specs=[pl.BlockSpec((CHUNK, d_inner), lambda b, t: (row(b, t), 0)),
                  pl.BlockSpec((CHUNK, gn), lambda b, t: (row(b, t), bblk)),
                  pl.BlockSpec((CHUNK, gn), lambda b, t: (row(b, t), bblk + 1)),
                  pl.BlockSpec((CHUNK, LANES), lambda b, t: (row(b, t), 0)),
                  pl.BlockSpec((1, LANES), lambda b, t: (0, 0)),
                  pl.BlockSpec((1, LANES), lambda b, t: (0, 0)),
                  state_spec],
        out_specs=[pl.BlockSpec((CHUNK, d_inner), lambda b, t: (row(b, t), 0)), state_spec],
        out_shape=[jax.ShapeDtypeStruct((batch * seq_len, d_inner), F32),
                   jax.ShapeDtypeStruct(h0.shape, F32)],
        scratch_shapes=[pltpu.VMEM((SSM_GROUPS, SSM_STATE, gw), F32)],
        compiler_params=_cparams(("parallel", "arbitrary"), 32 << 20),
        name="ssd_scan_bwd" if reverse else "ssd_scan_fwd",
    )(xbc, xbc, xbc, dt_raw, dt_bias_row, a_log_row, h0)


def _ssm_out_kernel(yf_ref, yb_ref, x_ref, z_ref, d_ref, g_ref, o_ref):
    gw = SSM_HEADS_PER_GROUP * SSM_HEAD_DIM
    for g in range(SSM_GROUPS):
        cols = slice(g * gw, (g + 1) * gw)
        y = yf_ref[:, cols] + yb_ref[:, cols] + d_ref[:, cols] * x_ref[:, cols]
        y = y * _silu(z_ref[:, cols])
        y = y * lax.rsqrt(jnp.mean(y * y, axis=-1, keepdims=True) + EPS) * g_ref[:, cols]
        o_ref[:, cols] = y.astype(o_ref.dtype)


def ssm_out(y_f, y_b, xbc, proj, z_blk, d_row, g_norm):
    m, d_inner = y_f.shape
    tr = _tile(m, 256)
    blk = lambda c: pl.BlockSpec((tr, d_inner), lambda i: (i, c))
    row = pl.BlockSpec((1, d_inner), lambda i: (0, 0))
    return pl.pallas_call(
        _ssm_out_kernel,
        grid=(m // tr,),
        in_specs=[blk(0), blk(0), blk(0), blk(z_blk), row, row],
        out_specs=blk(0),
        out_shape=jax.ShapeDtypeStruct((m, d_inner), BF16),
        compiler_params=_cparams(("parallel",), 14 * tr * d_inner * 4),
        name="ssm_out",
    )(y_f, y_b, xbc, proj, d_row, g_norm.reshape(1, d_inner))


def _head_norm(t, g_row):
    lane = lax.broadcasted_iota(jnp.int32, t.shape, 1)
    lo = lane < ATTN_QK_DIM
    sq = t * t
    s_lo = jnp.sum(jnp.where(lo, sq, 0.0), axis=-1, keepdims=True)
    s_hi = jnp.sum(jnp.where(lo, 0.0, sq), axis=-1, keepdims=True)
    ms = jnp.where(lo, s_lo, s_hi) * (1.0 / ATTN_QK_DIM)
    return t * lax.rsqrt(ms + EPS) * g_row


def _rope(t, cos, sin_signed):
    lane = lax.broadcasted_iota(jnp.int32, t.shape, 1)
    swapped = jnp.where(lane % 2 == 0, pltpu.roll(t, LANES - 1, 1), pltpu.roll(t, 1, 1))
    return t * cos + swapped * sin_signed


def _qkv_prep_kernel(*refs, use_rope):
    if use_rope:
        q_ref, k_ref, v_ref, gq_ref, gk_ref, cos_ref, sin_ref, qo_ref, ko_ref, vo_ref = refs
    else:
        q_ref, k_ref, v_ref, gq_ref, gk_ref, qo_ref, ko_ref, vo_ref = refs
    vo_ref[...] = v_ref[...].astype(vo_ref.dtype)
    for h in range(ATTN_HEADS):
        cols = slice(h * LANES, (h + 1) * LANES)
        qh = _head_norm(q_ref[:, cols], gq_ref[...])
        kh = _head_norm(k_ref[:, cols], gk_ref[...])
        if use_rope:
            qh = _rope(qh, cos_ref[...], sin_ref[...])
            kh = _rope(kh, cos_ref[...], sin_ref[...])
        qo_ref[:, cols] = (qh * (ATTN_QK_DIM ** -0.5)).astype(qo_ref.dtype)
        ko_ref[:, cols] = kh.astype(ko_ref.dtype)


def qkv_prep(proj, q_blk, width, seq_len, gq, gk, rope):
    m = proj.shape[0]
    tr = _tile(seq_len, 256)
    tiles_per_seq = seq_len // tr
    blk = lambda c: pl.BlockSpec((tr, width), lambda i: (i, c))
    row = pl.BlockSpec((1, LANES), lambda i: (0, 0))
    in_specs = [blk(q_blk), blk(q_blk + 1), blk(q_blk + 2), row, row]
    args = [proj, proj, proj, jnp.tile(gq, 2).reshape(1, LANES), jnp.tile(gk, 2).reshape(1, LANES)]
    if rope is not None:
        tab = pl.BlockSpec((tr, LANES), lambda i: (i % tiles_per_seq, 0))
        in_specs += [tab, tab]
        args += list(rope)
    out = jax.ShapeDtypeStruct((m, width), BF16)
    return pl.pallas_call(
        functools.partial(_qkv_prep_kernel, use_rope=rope is not None),
        grid=(m // tr,),
        in_specs=in_specs,
        out_specs=[blk(0), blk(0), blk(0)],
        out_shape=[out, out, out],
        compiler_params=_cparams(("parallel",), 16 * tr * width * 4),
        name="qkv_prep",
    )(*args)


def _attn_kernel(*refs, n_seg, lam_init):
    q_ref = refs[0]
    kv_refs = refs[1:1 + 2 * n_seg]
    lq_ref, lk_ref, g_ref, o_ref = refs[1 + 2 * n_seg:]
    lam = (jnp.exp(jnp.sum(lq_ref[0:1, :] * lk_ref[0:1, :], keepdims=True))
           - jnp.exp(jnp.sum(lq_ref[1:2, :] * lk_ref[1:2, :], keepdims=True)) + lam_init)
    q = q_ref[...]
    lane = lax.broadcasted_iota(jnp.int32, q.shape, 1)
    weights = []
    for m_idx in range(2):
        in_map = (lane < ATTN_QK_DIM) if m_idx == 0 else (lane >= ATTN_QK_DIM)
        qm = jnp.where(in_map, q, jnp.zeros_like(q))
        s = [lax.dot_general(qm, kv_refs[2 * i][...], (((1,), (1,)), ((), ())),
                             preferred_element_type=F32) for i in range(n_seg)]
        mx = functools.reduce(jnp.maximum, [jnp.max(si, axis=-1, keepdims=True) for si in s])
        p = [jnp.exp(si - mx) for si in s]
        denom = functools.reduce(jnp.add, [jnp.sum(pi, axis=-1, keepdims=True) for pi in p])
        weights.append((p, 1.0 / denom))
    (p1, r1), (p2, r2) = weights
    r2 = r2 * lam
    o = None
    for i in range(n_seg):
        w = (p1[i] * r1 - p2[i] * r2).astype(BF16)
        part = _dot(w, kv_refs[2 * i + 1][...])
        o = part if o is None else o + part
    o = o * lax.rsqrt(jnp.mean(o * o, axis=-1, keepdims=True) + EPS) * g_ref[...]
    o_ref[...] = (o * (1.0 - lam_init)).astype(o_ref.dtype)


def diff_attention(q, kv_segments, batch, lam_q, lam_k, subln_g, lam_init):
    mq, width = q.shape
    lq = mq // batch
    tq = _tile(lq, 256)
    nq = lq // tq
    in_specs = [pl.BlockSpec((tq, LANES), lambda b, h, i: (b * nq + i, h))]
    args = [q]
    for k, v, seg_len in kv_segments:
        spec = pl.BlockSpec((seg_len, LANES), lambda b, h, i: (b, h))
        in_specs += [spec, spec]
        args += [k, v]
    small = pl.BlockSpec((2, ATTN_QK_DIM), lambda b, h, i: (0, 0))
    in_specs += [small, small, pl.BlockSpec((1, LANES), lambda b, h, i: (0, 0))]
    args += [lam_q, lam_k, subln_g.reshape(1, LANES)]
    n_keys = sum(s for _, _, s in kv_segments)
    return pl.pallas_call(
        functools.partial(_attn_kernel, n_seg=len(kv_segments), lam_init=lam_init),
        grid=(batch, ATTN_HEADS, nq),
        in_specs=in_specs,
        out_specs=pl.BlockSpec((tq, LANES), lambda b, h, i: (b * nq + i, h)),
        out_shape=jax.ShapeDtypeStruct((mq, width), BF16),
        compiler_params=_cparams(("parallel", "parallel", "arbitrary"),
                                 8 * n_keys * LANES * 2 + 8 * tq * n_keys * 4 + (4 << 20)),
        name="diff_attention",
    )(*args)


def _rope_tables(seq_len):
    pos = jnp.arange(seq_len)
    row, col = pos // GRID_W, pos % GRID_W
    n_freq = ATTN_QK_DIM // 4
    inv = ROPE_BASE ** (-jnp.arange(n_freq, dtype=F32) / n_freq)
    ang = jnp.concatenate([row[:, None] * inv, col[:, None] * inv], axis=-1)
    cos = jnp.repeat(jnp.cos(ang), 2, axis=-1)
    sin = jnp.repeat(jnp.sin(ang), 2, axis=-1) * jnp.tile(jnp.array([-1.0, 1.0], F32), ATTN_QK_DIM // 2)
    return jnp.tile(cos, (1, 2)), jnp.tile(sin, (1, 2))


def _pad_lanes(v):
    return jnp.pad(v.reshape(1, -1), ((0, 0), (0, LANES - v.size)))


def kernel(x, c, ctx, c_ctx, ada_down, ada_up, ada_bias, norm1_g, w_in, gm_norm_g, gm_w, gm_b,
           conv_w, conv_b, dt_bias, a_log, d_skip, ssm_norm_g, q_norm_g, k_norm_g, lam_q, lam_k,
           subln_g, w_out, norm2_g, w_ffn_in, w_ffn_out):
    batch, seq, d = x.shape
    lc = ctx.shape[1]
    depth = w_in.shape[0]
    w_gmlp, w_ssm, w_attn = d // 4, d // 2, d // 4
    n_heads = w_ssm // SSM_HEAD_DIM
    gn = SSM_GROUPS * SSM_STATE
    conv_ch = w_ssm + 2 * gn
    assert batch < ADA_ROWS and 2 * n_heads <= LANES
    assert n_heads == SSM_GROUPS * SSM_HEADS_PER_GROUP and w_attn == ATTN_HEADS * LANES
    assert seq % CHUNK == 0 and lc % CHUNK == 0 and seq % GRID_W == 0
    off_dt = 2 * w_gmlp + w_ssm + conv_ch
    off_q = off_dt + 2 * n_heads

    xs = x.reshape(batch * seq, d)
    ss = ctx.reshape(batch * lc, d)
    cc = jnp.zeros((ADA_ROWS, d), F32).at[:batch].set(c).at[batch].set(c_ctx)
    rope = _rope_tables(seq)
    h_zero = jnp.zeros((batch, SSM_GROUPS, SSM_STATE, SSM_HEADS_PER_GROUP * SSM_HEAD_DIM), F32)
    lat = dict(rows_per_batch=seq, fixed_row=None)
    con = dict(rows_per_batch=None, fixed_row=batch)

    for l in range(depth):
        last = l == depth - 1
        lam_init = 0.8 - 0.6 * math.exp(-0.3 * l)
        w_main = jnp.concatenate([w_in[l][:, :off_dt], w_in[l][:, off_q:]], axis=1).astype(BF16)
        w_dt = jnp.pad(w_in[l][:, off_dt:off_q], ((0, 0), (0, LANES - 2 * n_heads))).astype(BF16)
        w_o = w_out[l].astype(BF16)
        w_f1 = w_ffn_in[l].astype(BF16)
        w_f2 = w_ffn_out[l].astype(BF16)
        gm_w_b = gm_w[l].astype(BF16)
        gm_b_full = jnp.repeat(gm_b[l].T, w_gmlp // GMLP_GROUPS, axis=1)
        d_row = jnp.repeat(d_skip[l], SSM_HEAD_DIM).reshape(1, w_ssm)
        dtb_row = _pad_lanes(dt_bias[l])
        alog_row = _pad_lanes(a_log[l])

        mods = adaln(cc, ada_down[l], ada_up[l], ada_bias[l])

        def mixers_in(stream, seq_len, sel, use_rope):
            h = modnorm(stream, norm1_g[l], mods, 0, 1, **sel)
            proj = matmul(h, w_main, F32)
            dt_raw = matmul(h, w_dt, F32)
            xbc = conv_silu(proj, (2 * w_gmlp + w_ssm) // conv_ch, conv_ch, seq_len, conv_w[l], conv_b[l])
            q, k, v = qkv_prep(proj, off_dt // w_attn, w_attn, seq_len, q_norm_g[l], k_norm_g[l],
                               rope if use_rope else None)
            return proj, dt_raw, xbc, q, k, v

        def ssm(proj, dt_raw, xbc, seq_len, h0_f, h0_b):
            y_f, h_f = ssd_scan(xbc, dt_raw, dtb_row, alog_row, h0_f, batch, seq_len, w_ssm, False, 0)
            y_b, h_b = ssd_scan(xbc, dt_raw, dtb_row, alog_row, h0_b, batch, seq_len, w_ssm, True, n_heads)
            y = ssm_out(y_f, y_b, xbc, proj, 2 * w_gmlp // w_ssm, d_row, ssm_norm_g[l])
            return y, h_f, h_b

        def mixers_out(stream, a, y, o, sel):
            stream = matmul_gated_residual([a, y, o], w_o, stream, mods, 2, **sel)
            h = modnorm(stream, norm2_g[l], mods, 3, 4, **sel)
            act = matmul_swiglu(h, w_f1)
            return matmul_gated_residual([act], w_f2, stream, mods, 5, tn_pref=256,
                                         single_buffer_a=True, **sel)

        proj_c, dt_c, xbc_c, q_c, k_c, v_c = mixers_in(ss, lc, con, False)
        proj_l, dt_l, xbc_l, q_l, k_l, v_l = mixers_in(xs, seq, lat, True)

        y_c, hf_c, hb_c = ssm(proj_c, dt_c, xbc_c, lc, h_zero, h_zero)
        y_l, _, _ = ssm(proj_l, dt_l, xbc_l, seq, hf_c, hb_c)

        o_l = diff_attention(q_l, [(k_c, v_c, lc), (k_l, v_l, seq)], batch, lam_q[l], lam_k[l],
                             subln_g[l], lam_init)
        a_l = gmlp(proj_l, 0, 1, w_gmlp, gm_norm_g[l], gm_w_b, gm_b_full)
        xs = mixers_out(xs, a_l, y_l, o_l, lat)

        if not last:
            o_c = diff_attention(q_c, [(k_c, v_c, lc)], batch, lam_q[l], lam_k[l], subln_g[l], lam_init)
            a_c = gmlp(proj_c, 0, 1, w_gmlp, gm_norm_g[l], gm_w_b, gm_b_full)
            ss = mixers_out(ss, a_c, y_c, o_c, con)

    return xs.reshape(batch, seq, d)
```

```python
import functools
import math

import jax
import jax.numpy as jnp
from jax import lax
from jax.experimental import pallas as pl
from jax.experimental.pallas import tpu as pltpu

F32 = jnp.float32
BF16 = jnp.bfloat16

GRID_W = 64
CHUNK = 128
GMLP_GROUPS = 8
SSM_HEAD_DIM = 64
SSM_GROUPS = 8
SSM_STATE = 128
SSM_HEADS_PER_GROUP = 4
ATTN_HEADS = 8
ATTN_QK_DIM = 64
ROPE_BASE = 10000.0
EPS = 1e-6
ADA_ROWS = 8
Q_SCALE = ATTN_QK_DIM ** -0.5 * math.log2(math.e)

LANES = 128
VMEM_CAP = 56 * 1024 * 1024


def _tile(n, pref, mult=8):
    if n <= pref:
        return n
    t = (pref // mult) * mult
    while t > mult and n % t:
        t -= mult
    assert n % t == 0, (n, pref, mult)
    return t


def _cparams(sem, vmem_bytes):
    return pltpu.CompilerParams(dimension_semantics=sem,
                                vmem_limit_bytes=int(min(max(vmem_bytes, 16 << 20), VMEM_CAP)))


def _dot(a, b):
    return jnp.dot(a, b, preferred_element_type=F32)


def _split_bf16(x, terms):
    parts = []
    for _ in range(terms):
        p = x.astype(BF16)
        parts.append(p)
        x = x - p.astype(F32)
    return parts


def _dot_x3(a, b):
    ah, al = _split_bf16(a, 2)
    bh, bl = _split_bf16(b, 2)
    return _dot(ah, bh) + (_dot(ah, bl) + _dot(al, bh))


def _silu(x):
    return x * jax.nn.sigmoid(x)


def _gelu_tanh(x):
    return 0.5 * x * (1.0 + jnp.tanh(math.sqrt(2.0 / math.pi) * (x + 0.044715 * (x * x * x))))


def _softplus(x):
    return jnp.maximum(x, 0.0) + jnp.log1p(jnp.exp(-jnp.abs(x)))


def _mod_row(ref, tiles_per_row, fixed_row, axis=0):
    if tiles_per_row is None:
        return ref[fixed_row:fixed_row + 1, :]
    return ref[pl.ds(pl.program_id(axis) // tiles_per_row, 1), :]


def _adaln_kernel(cc_ref, down_ref, up_ref, bias_ref, o_ref, h_ref):
    @pl.when(pl.program_id(0) == 0)
    def _():
        h_ref[...] = _dot_x3(_silu(cc_ref[...]), down_ref[...])

    o_ref[...] = _dot_x3(h_ref[...], up_ref[...]) + bias_ref[...]


def adaln(cc, down, up, bias):
    d, r = down.shape
    n = up.shape[1]
    tn = _tile(n, 2048, LANES)
    return pl.pallas_call(
        _adaln_kernel,
        grid=(n // tn,),
        in_specs=[pl.BlockSpec((ADA_ROWS, d), lambda j: (0, 0)),
                  pl.BlockSpec((d, r), lambda j: (0, 0)),
                  pl.BlockSpec((r, tn), lambda j: (0, j)),
                  pl.BlockSpec((1, tn), lambda j: (0, j))],
        out_specs=pl.BlockSpec((ADA_ROWS, tn), lambda j: (0, j)),
        out_shape=jax.ShapeDtypeStruct((ADA_ROWS, n), F32),
        scratch_shapes=[pltpu.VMEM((ADA_ROWS, r), F32)],
        compiler_params=_cparams(("arbitrary",), 6 * d * r * 4 + 8 * r * tn * 4),
        name="adaln",
    )(cc, down, up, bias.reshape(1, n))


def _modnorm_kernel(x_ref, g_ref, sh_ref, sc_ref, o_ref, *, tiles_per_row, fixed_row):
    x = x_ref[...]
    y = x * lax.rsqrt(jnp.mean(x * x, axis=-1, keepdims=True) + EPS) * g_ref[...]
    sh = _mod_row(sh_ref, tiles_per_row, fixed_row)
    sc = _mod_row(sc_ref, tiles_per_row, fixed_row)
    o_ref[...] = (y * (1.0 + sc) + sh).astype(o_ref.dtype)


def modnorm(x, g, mods, shift_idx, scale_idx, rows_per_batch, fixed_row):
    m, d = x.shape
    tr = _tile(m if rows_per_batch is None else rows_per_batch, 256)
    tiles_per_row = None if rows_per_batch is None else rows_per_batch // tr
    return pl.pallas_call(
        functools.partial(_modnorm_kernel, tiles_per_row=tiles_per_row, fixed_row=fixed_row),
        grid=(m // tr,),
        in_specs=[pl.BlockSpec((tr, d), lambda i: (i, 0)),
                  pl.BlockSpec((1, d), lambda i: (0, 0)),
                  pl.BlockSpec((ADA_ROWS, d), lambda i: (0, shift_idx)),
                  pl.BlockSpec((ADA_ROWS, d), lambda i: (0, scale_idx))],
        out_specs=pl.BlockSpec((tr, d), lambda i: (i, 0)),
        out_shape=jax.ShapeDtypeStruct((m, d), BF16),
        compiler_params=_cparams(("parallel",), 8 * tr * d * 4),
        name="modnorm",
    )(x, g.reshape(1, d), mods, mods)


def _mm_kernel(a_ref, w_ref, o_ref):
    o_ref[...] = _dot(a_ref[...], w_ref[...]).astype(o_ref.dtype)


def matmul(a, w, layer, out_dtype, tm_pref=1024, tn_pref=1024):
    m, k = a.shape
    n = w.shape[2]
    tm, tn = _tile(m, tm_pref), _tile(n, tn_pref, LANES)
    osz = jnp.dtype(out_dtype).itemsize
    return pl.pallas_call(
        _mm_kernel,
        grid=(m // tm, n // tn),
        in_specs=[pl.BlockSpec((tm, k), lambda i, j: (i, 0)),
                  pl.BlockSpec((None, k, tn), lambda i, j: (layer, 0, j))],
        out_specs=pl.BlockSpec((tm, tn), lambda i, j: (i, j)),
        out_shape=jax.ShapeDtypeStruct((m, n), out_dtype),
        compiler_params=_cparams(("parallel", "parallel"),
                                 2 * (tm * k * 2 + k * tn * 2 + tm * tn * osz) + tm * tn * 4 + (4 << 20)),
        name="matmul",
    )(a, w)


def _mm_resid_kernel(*refs, n_a, k_sizes, tiles_per_row, fixed_row):
    a_refs = refs[:n_a]
    w_ref, r_ref, g_ref, o_ref = refs[n_a:]
    acc = None
    off = 0
    for a_ref, ks in zip(a_refs, k_sizes):
        part = _dot(a_ref[...], w_ref[off:off + ks, :])
        acc = part if acc is None else acc + part
        off += ks
    o_ref[...] = r_ref[...] + _mod_row(g_ref, tiles_per_row, fixed_row) * acc


def matmul_gated_residual(a_list, w, layer, resid, mods, gate_idx, rows_per_batch, fixed_row,
                          tm_pref=1024, tn_pref=1024, single_buffer_a=False):
    m, n = resid.shape
    k_sizes = tuple(a.shape[1] for a in a_list)
    k = sum(k_sizes)
    tm = _tile(m if rows_per_batch is None else rows_per_batch, tm_pref)
    tn = _tile(n, tn_pref, LANES)
    tiles_per_row = None if rows_per_batch is None else rows_per_batch // tm
    nblk = n // tn
    a_kw = dict(pipeline_mode=pl.Buffered(1)) if single_buffer_a else {}
    a_specs = [pl.BlockSpec((tm, ks), lambda i, j: (i, 0), **a_kw) for ks in k_sizes]
    a_bufs = 1 if single_buffer_a else 2
    return pl.pallas_call(
        functools.partial(_mm_resid_kernel, n_a=len(a_list), k_sizes=k_sizes,
                          tiles_per_row=tiles_per_row, fixed_row=fixed_row),
        grid=(m // tm, nblk),
        in_specs=a_specs + [pl.BlockSpec((None, k, tn), lambda i, j: (layer, 0, j)),
                            pl.BlockSpec((tm, tn), lambda i, j: (i, j)),
                            pl.BlockSpec((ADA_ROWS, tn), lambda i, j: (0, gate_idx * nblk + j))],
        out_specs=pl.BlockSpec((tm, tn), lambda i, j: (i, j)),
        out_shape=jax.ShapeDtypeStruct((m, n), F32),
        compiler_params=_cparams(("parallel", "parallel"),
                                 a_bufs * tm * k * 2 + 2 * k * tn * 2 + 5 * tm * tn * 4 + (4 << 20)),
        name="matmul_gated_residual",
    )(*a_list, w, resid, mods)


def _mm_swiglu_kernel(a_ref, wg_ref, wu_ref, o_ref):
    a = a_ref[...]
    g = _dot(a, wg_ref[...])
    u = _dot(a, wu_ref[...])
    o_ref[...] = (_silu(g) * u).astype(o_ref.dtype)


def matmul_swiglu(a, w, layer, tm_pref=2048, tn_pref=256):
    m, k = a.shape
    hdim = w.shape[2] // 2
    tm, tn = _tile(m, tm_pref), _tile(hdim, tn_pref, LANES)
    nblk = hdim // tn
    return pl.pallas_call(
        _mm_swiglu_kernel,
        grid=(m // tm, nblk),
        in_specs=[pl.BlockSpec((tm, k), lambda i, j: (i, 0)),
                  pl.BlockSpec((None, k, tn), lambda i, j: (layer, 0, j)),
                  pl.BlockSpec((None, k, tn), lambda i, j: (layer, 0, nblk + j))],
        out_specs=pl.BlockSpec((tm, tn), lambda i, j: (i, j)),
        out_shape=jax.ShapeDtypeStruct((m, hdim), BF16),
        compiler_params=_cparams(("parallel", "parallel"),
                                 2 * (tm * k * 2 + 2 * k * tn * 2 + tm * tn * 2) + 4 * tm * tn * 4 + (4 << 20)),
        name="matmul_swiglu",
    )(a, w, w)


def _gmlp_kernel(gu_ref, gv_ref, gn_ref, w_ref, b_ref, o_ref, *, group_w):
    v = _gelu_tanh(gv_ref[...])
    v = (v * lax.rsqrt(jnp.mean(v * v, axis=-1, keepdims=True) + EPS) * gn_ref[...]).astype(BF16)
    for n in range(gu_ref.shape[0] // CHUNK):
        rows = slice(n * CHUNK, (n + 1) * CHUNK)
        for g in range(GMLP_GROUPS):
            cols = slice(g * group_w, (g + 1) * group_w)
            s = _dot(w_ref[g], v[rows, cols]) + b_ref[:, cols]
            o_ref[rows, cols] = (_gelu_tanh(gu_ref[rows, cols]) * s).astype(o_ref.dtype)


def gmlp(proj, gu_blk, gv_blk, width, gn, w_s, layer, b_full):
    m = proj.shape[0]
    tr = _tile(m, 512, CHUNK)
    return pl.pallas_call(
        functools.partial(_gmlp_kernel, group_w=width // GMLP_GROUPS),
        grid=(m // tr,),
        in_specs=[pl.BlockSpec((tr, width), lambda i: (i, gu_blk)),
                  pl.BlockSpec((tr, width), lambda i: (i, gv_blk)),
                  pl.BlockSpec((1, width), lambda i: (0, 0)),
                  pl.BlockSpec((None, GMLP_GROUPS, CHUNK, CHUNK), lambda i: (layer, 0, 0, 0)),
                  pl.BlockSpec((CHUNK, width), lambda i: (0, 0))],
        out_specs=pl.BlockSpec((tr, width), lambda i: (i, 0)),
        out_shape=jax.ShapeDtypeStruct((m, width), BF16),
        compiler_params=_cparams(("parallel",), 12 * tr * width * 4),
        name="gmlp",
    )(proj, proj, gn.reshape(1, width), w_s, b_full)


def _conv_kernel(x_ref, prev_ref, next_ref, w_ref, b_ref, o_ref, *, tiles_per_seq):
    t = pl.program_id(0) % tiles_per_seq
    x = x_ref[...]
    rows = x.shape[0]
    rid = lax.broadcasted_iota(jnp.int32, x.shape, 0)
    prev_row = jnp.where(t == 0, 0.0, prev_ref[7:8, :])
    next_row = jnp.where(t == tiles_per_seq - 1, 0.0, next_ref[0:1, :])
    x_m1 = jnp.where(rid == 0, prev_row, pltpu.roll(x, 1, 0))
    x_p1 = jnp.where(rid == rows - 1, next_row, pltpu.roll(x, rows - 1, 0))
    y = w_ref[0:1, :] * x_m1 + w_ref[1:2, :] * x + w_ref[2:3, :] * x_p1 + b_ref[...]
    o_ref[...] = _silu(y)


def conv_silu(proj, col_blk, width, seq_len, w, b):
    m = proj.shape[0]
    tr = _tile(seq_len, 256)
    tiles_per_seq = seq_len // tr
    hb = tr // 8
    last = m // 8 - 1
    return pl.pallas_call(
        functools.partial(_conv_kernel, tiles_per_seq=tiles_per_seq),
        grid=(m // tr,),
        in_specs=[pl.BlockSpec((tr, width), lambda i: (i, col_blk)),
                  pl.BlockSpec((8, width), lambda i: (jnp.maximum(i * hb - 1, 0), col_blk)),
                  pl.BlockSpec((8, width), lambda i: (jnp.minimum((i + 1) * hb, last), col_blk)),
                  pl.BlockSpec((3, width), lambda i: (0, 0)),
                  pl.BlockSpec((1, width), lambda i: (0, 0))],
        out_specs=pl.BlockSpec((tr, width), lambda i: (i, 0)),
        out_shape=jax.ShapeDtypeStruct((m, width), F32),
        compiler_params=_cparams(("parallel",), 10 * tr * width * 4),
        name="conv_silu",
    )(proj, proj, proj, w, b.reshape(1, width))


def _ssd_kernel(*refs, reverse, dcol, fuse_out):
    if fuse_out:
        (x_ref, b_ref, c_ref, dt_ref, dtb_ref, alog_ref, sel_ref, h0_ref, yo_ref, z_ref, d_ref,
         gn_ref, y_ref, hfin_ref, h_scr) = refs
    else:
        (x_ref, b_ref, c_ref, dt_ref, dtb_ref, alog_ref, sel_ref, h0_ref,
         y_ref, hfin_ref, h_scr) = refs
    t = pl.program_id(1)
    q = CHUNK
    gw = SSM_HEADS_PER_GROUP * SSM_HEAD_DIM

    @pl.when(t == 0)
    def _():
        h_scr[...] = h0_ref[...]

    dt = _softplus(dt_ref[...] + dtb_ref[...])
    a = dt * (-jnp.exp(alog_ref[...]))
    ii = lax.broadcasted_iota(jnp.int32, (q, q), 0)
    jj = lax.broadcasted_iota(jnp.int32, (q, q), 1)
    tri = (jj >= ii) if reverse else (jj <= ii)
    tri_b = tri.astype(BF16)
    a_parts = _split_bf16(a, 3)
    ac = _dot(tri_b, a_parts[0]) + (_dot(tri_b, a_parts[1]) + _dot(tri_b, a_parts[2]))
    ac_t = ac.T
    edge = 0 if reverse else q - 1
    ac_end = ac[edge:edge + 1, :]

    def expand(mat):
        hi, lo = _split_bf16(mat, 2)
        return _dot(hi, sel_ref[...]) + _dot(lo, sel_ref[...])

    dt_x = expand(dt)
    grow_x = expand(jnp.exp(ac))
    wend_x = expand(dt * jnp.exp(ac_end - ac))
    lane = lax.broadcasted_iota(jnp.int32, (q, LANES), 1)

    for g in range(SSM_GROUPS):
        cols = slice(g * gw, (g + 1) * gw)
        xg = x_ref[:, cols]
        bg = b_ref[:, g * SSM_STATE:(g + 1) * SSM_STATE].astype(BF16)
        cg = c_ref[:, g * SSM_STATE:(g + 1) * SSM_STATE].astype(BF16)
        cb = lax.dot_general(cg, bg, (((1,), (1,)), ((), ())), preferred_element_type=F32)
        xdt = xg * dt_x[:, cols]
        h_prev = h_scr[g]
        y_parts = []
        for pair in range(SSM_HEADS_PER_GROUP // 2):
            ms = []
            for e in (2 * pair, 2 * pair + 1):
                c = dcol + g * SSM_HEADS_PER_GROUP + e
                seg = jnp.broadcast_to(ac[:, c:c + 1], (q, q)) - jnp.broadcast_to(ac_t[c:c + 1, :], (q, q))
                ms.append((cb * jnp.exp(jnp.where(tri, seg, -jnp.inf))).astype(BF16))
            xp = xdt[:, pair * LANES:(pair + 1) * LANES]
            rhs = jnp.concatenate([jnp.where(lane < SSM_HEAD_DIM, xp, 0.0),
                                   jnp.where(lane < SSM_HEAD_DIM, 0.0, xp)], axis=0).astype(BF16)
            y_parts.append(_dot(jnp.concatenate(ms, axis=1), rhs))
        y = jnp.concatenate(y_parts, axis=1) + _dot(cg, h_prev.astype(BF16)) * grow_x[:, cols]
        xdd = (xg * wend_x[:, cols]).astype(BF16)
        s_t = lax.dot_general(bg, xdd, (((0,), (0,)), ((), ())), preferred_element_type=F32)
        h_scr[g] = h_prev * grow_x[edge:edge + 1, cols] + s_t
        if fuse_out:
            y = (y + yo_ref[:, cols] + d_ref[:, cols] * xg) * _silu(z_ref[:, cols])
            y = y * lax.rsqrt(jnp.mean(y * y, axis=-1, keepdims=True) + EPS) * gn_ref[:, cols]
        y_ref[:, cols] = y.astype(y_ref.dtype)

    @pl.when(t == pl.num_programs(1) - 1)
    def _():
        hfin_ref[...] = h_scr[...]


def ssd_scan(xbc, dt_raw, dt_bias_row, a_log_row, h0, batch, seq_len, d_inner, reverse, dcol,
             out_stage=None):
    nchunks = seq_len // CHUNK
    gn = SSM_GROUPS * SSM_STATE
    gw = SSM_HEADS_PER_GROUP * SSM_HEAD_DIM
    bblk = d_inner // gn
    if reverse:
        row = lambda b, t: b * nchunks + (nchunks - 1 - t)
    else:
        row = lambda b, t: b * nchunks + t
    head_of_lane = dcol + jnp.arange(d_inner) // SSM_HEAD_DIM
    sel = (jnp.arange(LANES)[:, None] == head_of_lane[None, :]).astype(BF16)
    state_spec = pl.BlockSpec((None, SSM_GROUPS, SSM_STATE, gw), lambda b, t: (b, 0, 0, 0))
    wide = lambda c: pl.BlockSpec((CHUNK, d_inner), lambda b, t: (row(b, t), c))
    row_spec = pl.BlockSpec((1, LANES), lambda b, t: (0, 0))
    in_specs = [wide(0),
                pl.BlockSpec((CHUNK, gn), lambda b, t: (row(b, t), bblk)),
                pl.BlockSpec((CHUNK, gn), lambda b, t: (row(b, t), bblk + 1)),
                pl.BlockSpec((CHUNK, LANES), lambda b, t: (row(b, t), 0)),
                row_spec, row_spec,
                pl.BlockSpec((LANES, d_inner), lambda b, t: (0, 0)),
                state_spec]
    args = [xbc, xbc, xbc, dt_raw, dt_bias_row, a_log_row, sel, h0]
    if out_stage is not None:
        y_other, proj, z_blk, d_row, g_norm = out_stage
        full_row = pl.BlockSpec((1, d_inner), lambda b, t: (0, 0))
        in_specs += [wide(0), wide(z_blk), full_row, full_row]
        args += [y_other, proj, d_row, g_norm.reshape(1, d_inner)]
    return pl.pallas_call(
        functools.partial(_ssd_kernel, reverse=reverse, dcol=dcol, fuse_out=out_stage is not None),
        grid=(batch, nchunks),
        in_specs=in_specs,
        out_specs=[wide(0), state_spec],
        out_shape=[jax.ShapeDtypeStruct((batch * seq_len, d_inner), F32 if out_stage is None else BF16),
                   jax.ShapeDtypeStruct(h0.shape, F32)],
        scratch_shapes=[pltpu.VMEM((SSM_GROUPS, SSM_STATE, gw), F32)],
        compiler_params=_cparams(("parallel", "arbitrary"), 40 << 20),
        name="ssd_scan_bwd" if reverse else "ssd_scan_fwd",
    )(*args)


def _head_norm(t, g_row):
    lane = lax.broadcasted_iota(jnp.int32, t.shape, 1)
    lo = lane < ATTN_QK_DIM
    sq = t * t
    s_lo = jnp.sum(jnp.where(lo, sq, 0.0), axis=-1, keepdims=True)
    s_hi = jnp.sum(jnp.where(lo, 0.0, sq), axis=-1, keepdims=True)
    ms = jnp.where(lo, s_lo, s_hi) * (1.0 / ATTN_QK_DIM)
    return t * lax.rsqrt(ms + EPS) * g_row


def _rope(t, cos, sin_signed):
    lane = lax.broadcasted_iota(jnp.int32, t.shape, 1)
    swapped = jnp.where(lane % 2 == 0, pltpu.roll(t, LANES - 1, 1), pltpu.roll(t, 1, 1))
    return t * cos + swapped * sin_signed


def _qkv_prep_kernel(*refs, use_rope):
    if use_rope:
        q_ref, k_ref, v_ref, gq_ref, gk_ref, cos_ref, sin_ref, qo_ref, ko_ref, vo_ref = refs
    else:
        q_ref, k_ref, v_ref, gq_ref, gk_ref, qo_ref, ko_ref, vo_ref = refs
    vo_ref[...] = v_ref[...].astype(vo_ref.dtype)
    for h in range(ATTN_HEADS):
        cols = slice(h * LANES, (h + 1) * LANES)
        qh = _head_norm(q_ref[:, cols], gq_ref[...])
        kh = _head_norm(k_ref[:, cols], gk_ref[...])
        if use_rope:
            qh = _rope(qh, cos_ref[...], sin_ref[...])
            kh = _rope(kh, cos_ref[...], sin_ref[...])
        qo_ref[:, cols] = (qh * Q_SCALE).astype(qo_ref.dtype)
        ko_ref[:, cols] = kh.astype(ko_ref.dtype)


def qkv_prep(proj, q_blk, width, seq_len, gq, gk, rope):
    m = proj.shape[0]
    tr = _tile(seq_len, 256)
    tiles_per_seq = seq_len // tr
    blk = lambda c: pl.BlockSpec((tr, width), lambda i: (i, c))
    row = pl.BlockSpec((1, LANES), lambda i: (0, 0))
    in_specs = [blk(q_blk), blk(q_blk + 1), blk(q_blk + 2), row, row]
    args = [proj, proj, proj, jnp.tile(gq, 2).reshape(1, LANES), jnp.tile(gk, 2).reshape(1, LANES)]
    if rope is not None:
        tab = pl.BlockSpec((tr, LANES), lambda i: (i % tiles_per_seq, 0))
        in_specs += [tab, tab]
        args += list(rope)
    out = jax.ShapeDtypeStruct((m, width), BF16)
    return pl.pallas_call(
        functools.partial(_qkv_prep_kernel, use_rope=rope is not None),
        grid=(m // tr,),
        in_specs=in_specs,
        out_specs=[blk(0), blk(0), blk(0)],
        out_shape=[out, out, out],
        compiler_params=_cparams(("parallel",), 16 * tr * width * 4),
        name="qkv_prep",
    )(*args)


def _attn_kernel(*refs, n_seg, seg_lens, tq, n_blk, key_blk, lam_init):
    q_ref = refs[0]
    kv_refs = refs[1:1 + 2 * n_seg]
    lq_ref, lk_ref, g_ref, o_ref, s_a, s_b, p_a, p_b = refs[1 + 2 * n_seg:]
    lam = (jnp.exp(jnp.sum(lq_ref[0:1, :] * lk_ref[0:1, :], keepdims=True))
           - jnp.exp(jnp.sum(lq_ref[1:2, :] * lk_ref[1:2, :], keepdims=True)) + lam_init)
    lane = lax.broadcasted_iota(jnp.int32, (tq, LANES), 1)
    blocks = []
    off = 0
    for i in range(n_seg):
        blocks += [(i, c0, off + c0) for c0 in range(0, seg_lens[i], key_blk)]
        off += seg_lens[i]
    lane_blocks = range(0, key_blk, LANES)
    n_keys = sum(seg_lens)
    sm_rows = min(tq, 64)

    def score_items(q, s_buf):
        items = []
        for m_idx in range(2):
            in_map = (lane < ATTN_QK_DIM) if m_idx == 0 else (lane >= ATTN_QK_DIM)
            qm = jnp.where(in_map, q, jnp.zeros_like(q))

            def item(qm=qm, m_idx=m_idx, i=0, c0=0, g0=0):
                s_buf[m_idx, :, g0:g0 + key_blk] = lax.dot_general(
                    qm, kv_refs[2 * i][c0:c0 + key_blk, :], (((1,), (1,)), ((), ())),
                    preferred_element_type=F32)

            items += [functools.partial(item, i=i, c0=c0, g0=g0) for i, c0, g0 in blocks]
        return items

    def interleave(vpu_items, mxu_items):
        done = 0
        for n, f in enumerate(vpu_items):
            f()
            upto = (n + 1) * len(mxu_items) // len(vpu_items)
            for g in mxu_items[done:upto]:
                g()
            done = upto

    def phase(s_buf, p_buf, next_items):
        denoms = {}
        lane_cols = [slice(c, c + LANES) for c in range(0, n_keys, LANES)]

        def softmax_item(m_idx, r0):
            rows = slice(r0, r0 + sm_rows)
            mx = functools.reduce(jnp.maximum, [s_buf[m_idx, rows, cols] for cols in lane_cols])
            mxb = jnp.broadcast_to(jnp.max(mx, axis=-1, keepdims=True), (sm_rows, LANES))
            acc = None
            for cols in lane_cols:
                p = jnp.exp2(s_buf[m_idx, rows, cols] - mxb)
                acc = p if acc is None else acc + p
                p_buf[m_idx, rows, cols] = p.astype(BF16)
            denoms[m_idx, r0] = jnp.sum(acc, axis=-1, keepdims=True)

        row_starts = range(0, tq, sm_rows)
        interleave([functools.partial(softmax_item, m_idx, r0) for r0 in row_starts for m_idx in range(2)],
                   next_items)
        denom = [jnp.concatenate([denoms[m_idx, r0] for r0 in row_starts], axis=0) for m_idx in range(2)]
        r1 = jnp.broadcast_to(1.0 / denom[0], (tq, LANES)).astype(BF16)
        r2 = jnp.broadcast_to(lam / denom[1], (tq, LANES)).astype(BF16)
        o = None
        for i, c0, g0 in blocks:
            ws = [p_buf[0, :, g0 + c:g0 + c + LANES] * r1 - p_buf[1, :, g0 + c:g0 + c + LANES] * r2
                  for c in lane_blocks]
            part = _dot(jnp.concatenate(ws, axis=1), kv_refs[2 * i + 1][c0:c0 + key_blk, :])
            o = part if o is None else o + part
        o = o * lax.rsqrt(jnp.mean(o * o, axis=-1, keepdims=True) + EPS) * g_ref[...]
        return (o * (1.0 - lam_init)).astype(o_ref.dtype)

    bufs = [(s_a, p_a), (s_b, p_b)]
    for f in score_items(q_ref[0:tq, :], s_a):
        f()
    for j in range(n_blk):
        s_cur, p_cur = bufs[j % 2]
        nxt = score_items(q_ref[(j + 1) * tq:(j + 2) * tq, :], bufs[(j + 1) % 2][0]) if j + 1 < n_blk else []
        o_ref[j * tq:(j + 1) * tq, :] = phase(s_cur, p_cur, nxt)


def diff_attention(q, kv_segments, batch, lam_q, lam_k, subln_g, lam_init):
    mq, width = q.shape
    lq = mq // batch
    tq = _tile(lq // 2, 256, 16)
    n_blk = min(lq // tq, 4)
    steps = lq // (n_blk * tq)
    assert steps * n_blk * tq == lq
    seg_lens = tuple(s for _, _, s in kv_segments)
    n_keys = sum(seg_lens)
    key_blk = 256 if all(s % 256 == 0 for s in seg_lens) else LANES
    q_spec = pl.BlockSpec((n_blk * tq, LANES), lambda b, h, i: (b * steps + i, h))
    in_specs = [q_spec]
    args = [q]
    for k, v, seg_len in kv_segments:
        spec = pl.BlockSpec((seg_len, LANES), lambda b, h, i: (b, h))
        in_specs += [spec, spec]
        args += [k, v]
    small = pl.BlockSpec((2, ATTN_QK_DIM), lambda b, h, i: (0, 0))
    in_specs += [small, small, pl.BlockSpec((1, LANES), lambda b, h, i: (0, 0))]
    args += [lam_q, lam_k, subln_g.reshape(1, LANES)]
    return pl.pallas_call(
        functools.partial(_attn_kernel, n_seg=len(kv_segments), seg_lens=seg_lens, tq=tq,
                          n_blk=n_blk, key_blk=key_blk, lam_init=lam_init),
        grid=(batch, ATTN_HEADS, steps),
        in_specs=in_specs,
        out_specs=q_spec,
        out_shape=jax.ShapeDtypeStruct((mq, width), BF16),
        scratch_shapes=[pltpu.VMEM((2, tq, n_keys), F32), pltpu.VMEM((2, tq, n_keys), F32),
                        pltpu.VMEM((2, tq, n_keys), BF16), pltpu.VMEM((2, tq, n_keys), BF16)],
        compiler_params=_cparams(("parallel", "parallel", "arbitrary"),
                                 8 * n_keys * LANES * 2 + 2 * tq * n_keys * (12 + 12) + (6 << 20)),
        name="diff_attention",
    )(*args)


def _rope_tables(seq_len):
    pos = jnp.arange(seq_len)
    row, col = pos // GRID_W, pos % GRID_W
    n_freq = ATTN_QK_DIM // 4
    inv = ROPE_BASE ** (-jnp.arange(n_freq, dtype=F32) / n_freq)
    ang = jnp.concatenate([row[:, None] * inv, col[:, None] * inv], axis=-1)
    cos = jnp.repeat(jnp.cos(ang), 2, axis=-1)
    sin = jnp.repeat(jnp.sin(ang), 2, axis=-1) * jnp.tile(jnp.array([-1.0, 1.0], F32), ATTN_QK_DIM // 2)
    return jnp.tile(cos, (1, 2)), jnp.tile(sin, (1, 2))


def _pad_lanes(v):
    return jnp.pad(v.reshape(1, -1), ((0, 0), (0, LANES - v.size)))


def kernel(x, c, ctx, c_ctx, ada_down, ada_up, ada_bias, norm1_g, w_in, gm_norm_g, gm_w, gm_b,
           conv_w, conv_b, dt_bias, a_log, d_skip, ssm_norm_g, q_norm_g, k_norm_g, lam_q, lam_k,
           subln_g, w_out, norm2_g, w_ffn_in, w_ffn_out):
    batch, seq, d = x.shape
    lc = ctx.shape[1]
    depth = w_in.shape[0]
    w_gmlp, w_ssm, w_attn = d // 4, d // 2, d // 4
    n_heads = w_ssm // SSM_HEAD_DIM
    gn = SSM_GROUPS * SSM_STATE
    conv_ch = w_ssm + 2 * gn
    assert batch < ADA_ROWS and 2 * n_heads <= LANES
    assert n_heads == SSM_GROUPS * SSM_HEADS_PER_GROUP and w_attn == ATTN_HEADS * LANES
    assert seq % CHUNK == 0 and lc % CHUNK == 0 and seq % GRID_W == 0
    off_dt = 2 * w_gmlp + w_ssm + conv_ch
    off_q = off_dt + 2 * n_heads

    xs = x.reshape(batch * seq, d)
    ss = ctx.reshape(batch * lc, d)
    cc = jnp.zeros((ADA_ROWS, d), F32).at[:batch].set(c).at[batch].set(c_ctx)
    rope = _rope_tables(seq)
    h_zero = jnp.zeros((batch, SSM_GROUPS, SSM_STATE, SSM_HEADS_PER_GROUP * SSM_HEAD_DIM), F32)
    lat = dict(rows_per_batch=seq, fixed_row=None)
    con = dict(rows_per_batch=None, fixed_row=batch)

    w_main = w_in[:, :, :off_dt].astype(BF16)
    w_qkv = w_in[:, :, off_q:].astype(BF16)
    w_dt = jnp.pad(w_in[:, :, off_dt:off_q], ((0, 0), (0, 0), (0, LANES - 2 * n_heads))).astype(BF16)
    w_o = w_out.astype(BF16)
    w_f1 = w_ffn_in.astype(BF16)
    w_f2 = w_ffn_out.astype(BF16)
    gm_w_b = gm_w.astype(BF16)
    z_blk = 2 * w_gmlp // w_ssm
    xbc_blk = (2 * w_gmlp + w_ssm) // conv_ch

    for l in range(depth):
        last = l == depth - 1
        lam_init = 0.8 - 0.6 * math.exp(-0.3 * l)
        gm_b_full = jnp.repeat(gm_b[l].T, w_gmlp // GMLP_GROUPS, axis=1)
        d_row = jnp.repeat(d_skip[l], SSM_HEAD_DIM).reshape(1, w_ssm)
        dtb_row = _pad_lanes(dt_bias[l])
        alog_row = _pad_lanes(a_log[l])

        mods = adaln(cc, ada_down[l], ada_up[l], ada_bias[l])

        def mixers_in(stream, seq_len, sel, use_rope):
            h = modnorm(stream, norm1_g[l], mods, 0, 1, **sel)
            proj = matmul(h, w_main, l, F32)
            pqkv = matmul(h, w_qkv, l, F32)
            dt_raw = matmul(h, w_dt, l, F32)
            xbc = conv_silu(proj, xbc_blk, conv_ch, seq_len, conv_w[l], conv_b[l])
            q, k, v = qkv_prep(pqkv, 0, w_attn, seq_len, q_norm_g[l], k_norm_g[l],
                               rope if use_rope else None)
            return proj, dt_raw, xbc, q, k, v

        def ssm(proj, dt_raw, xbc, seq_len, h0_f, h0_b):
            y_f, h_f = ssd_scan(xbc, dt_raw, dtb_row, alog_row, h0_f, batch, seq_len, w_ssm, False, 0)
            y, h_b = ssd_scan(xbc, dt_raw, dtb_row, alog_row, h0_b, batch, seq_len, w_ssm, True, n_heads,
                              out_stage=(y_f, proj, z_blk, d_row, ssm_norm_g[l]))
            return y, h_f, h_b

        def mixers_out(stream, a, y, o, sel):
            stream = matmul_gated_residual([a, y, o], w_o, l, stream, mods, 2, **sel)
            h = modnorm(stream, norm2_g[l], mods, 3, 4, **sel)
            act = matmul_swiglu(h, w_f1, l)
            return matmul_gated_residual([act], w_f2, l, stream, mods, 5, tn_pref=256,
                                         single_buffer_a=True, **sel)

        proj_c, dt_c, xbc_c, q_c, k_c, v_c = mixers_in(ss, lc, con, False)
        proj_l, dt_l, xbc_l, q_l, k_l, v_l = mixers_in(xs, seq, lat, True)

        y_c, hf_c, hb_c = ssm(proj_c, dt_c, xbc_c, lc, h_zero, h_zero)
        y_l, _, _ = ssm(proj_l, dt_l, xbc_l, seq, hf_c, hb_c)

        o_l = diff_attention(q_l, [(k_c, v_c, lc), (k_l, v_l, seq)], batch, lam_q[l], lam_k[l],
                             subln_g[l], lam_init)
        a_l = gmlp(proj_l, 0, 1, w_gmlp, gm_norm_g[l], gm_w_b, l, gm_b_full)
        xs = mixers_out(xs, a_l, y_l, o_l, lat)

        if not last:
            o_c = diff_attention(q_c, [(k_c, v_c, lc)], batch, lam_q[l], lam_k[l], subln_g[l], lam_init)
            a_c = gmlp(proj_c, 0, 1, w_gmlp, gm_norm_g[l], gm_w_b, l, gm_b_full)
            ss = mixers_out(ss, a_c, y_c, o_c, con)

    return xs.reshape(batch, seq, d)
```

```python
import functools
import math

import jax
import jax.numpy as jnp
from jax import lax
from jax.experimental import pallas as pl
from jax.experimental.pallas import tpu as pltpu

F32 = jnp.float32
BF16 = jnp.bfloat16

GRID_W = 64
CHUNK = 128
GMLP_GROUPS = 8
SSM_HEAD_DIM = 64
SSM_GROUPS = 8
SSM_STATE = 128
SSM_HEADS_PER_GROUP = 4
ATTN_HEADS = 8
ATTN_QK_DIM = 64
ROPE_BASE = 10000.0
EPS = 1e-6
ADA_ROWS = 8
Q_SCALE = ATTN_QK_DIM ** -0.5 * math.log2(math.e)

LANES = 128
VMEM_CAP = 56 * 1024 * 1024


def _tile(n, pref, mult=8):
    if n <= pref:
        return n
    t = (pref // mult) * mult
    while t > mult and n % t:
        t -= mult
    assert n % t == 0, (n, pref, mult)
    return t


def _cparams(sem, vmem_bytes):
    return pltpu.CompilerParams(dimension_semantics=sem,
                                vmem_limit_bytes=int(min(max(vmem_bytes, 16 << 20), VMEM_CAP)))


def _dot(a, b):
    return jnp.dot(a, b, preferred_element_type=F32)


def _split_bf16(x, terms):
    parts = []
    for _ in range(terms):
        p = x.astype(BF16)
        parts.append(p)
        x = x - p.astype(F32)
    return parts


def _dot_x3(a, b):
    ah, al = _split_bf16(a, 2)
    bh, bl = _split_bf16(b, 2)
    return _dot(ah, bh) + (_dot(ah, bl) + _dot(al, bh))


def _silu(x):
    return x * jax.nn.sigmoid(x)


def _gelu_tanh(x):
    return 0.5 * x * (1.0 + jnp.tanh(math.sqrt(2.0 / math.pi) * (x + 0.044715 * (x * x * x))))


def _softplus(x):
    return jnp.maximum(x, 0.0) + jnp.log1p(jnp.exp(-jnp.abs(x)))


def _mod_row(ref, tiles_per_row, fixed_row, axis=0):
    if tiles_per_row is None:
        return ref[fixed_row:fixed_row + 1, :]
    return ref[pl.ds(pl.program_id(axis) // tiles_per_row, 1), :]


def _adaln_kernel(cc_ref, down_ref, up_ref, bias_ref, o_ref, h_ref):
    @pl.when(pl.program_id(0) == 0)
    def _():
        h_ref[...] = _dot_x3(_silu(cc_ref[...]), down_ref[...])

    o_ref[...] = _dot_x3(h_ref[...], up_ref[...]) + bias_ref[...]


def adaln(cc, down, up, bias):
    d, r = down.shape
    n = up.shape[1]
    tn = _tile(n, 2048, LANES)
    return pl.pallas_call(
        _adaln_kernel,
        grid=(n // tn,),
        in_specs=[pl.BlockSpec((ADA_ROWS, d), lambda j: (0, 0)),
                  pl.BlockSpec((d, r), lambda j: (0, 0)),
                  pl.BlockSpec((r, tn), lambda j: (0, j)),
                  pl.BlockSpec((1, tn), lambda j: (0, j))],
        out_specs=pl.BlockSpec((ADA_ROWS, tn), lambda j: (0, j)),
        out_shape=jax.ShapeDtypeStruct((ADA_ROWS, n), F32),
        scratch_shapes=[pltpu.VMEM((ADA_ROWS, r), F32)],
        compiler_params=_cparams(("arbitrary",), 6 * d * r * 4 + 8 * r * tn * 4),
        name="adaln",
    )(cc, down, up, bias.reshape(1, n))


def _modnorm_kernel(x_ref, g_ref, sh_ref, sc_ref, o_ref, *, tiles_per_row, fixed_row):
    x = x_ref[...]
    y = x * lax.rsqrt(jnp.mean(x * x, axis=-1, keepdims=True) + EPS) * g_ref[...]
    sh = _mod_row(sh_ref, tiles_per_row, fixed_row)
    sc = _mod_row(sc_ref, tiles_per_row, fixed_row)
    o_ref[...] = (y * (1.0 + sc) + sh).astype(o_ref.dtype)


def modnorm(x, g, mods, shift_idx, scale_idx, rows_per_batch, fixed_row):
    m, d = x.shape
    tr = _tile(m if rows_per_batch is None else rows_per_batch, 512)
    tiles_per_row = None if rows_per_batch is None else rows_per_batch // tr
    return pl.pallas_call(
        functools.partial(_modnorm_kernel, tiles_per_row=tiles_per_row, fixed_row=fixed_row),
        grid=(m // tr,),
        in_specs=[pl.BlockSpec((tr, d), lambda i: (i, 0)),
                  pl.BlockSpec((1, d), lambda i: (0, 0)),
                  pl.BlockSpec((ADA_ROWS, d), lambda i: (0, shift_idx)),
                  pl.BlockSpec((ADA_ROWS, d), lambda i: (0, scale_idx))],
        out_specs=pl.BlockSpec((tr, d), lambda i: (i, 0)),
        out_shape=jax.ShapeDtypeStruct((m, d), BF16),
        compiler_params=_cparams(("parallel",), 8 * tr * d * 4),
        name="modnorm",
    )(x, g.reshape(1, d), mods, mods)


def _mm_kernel(a_ref, w_ref, o_ref):
    o_ref[...] = _dot(a_ref[...], w_ref[...]).astype(o_ref.dtype)


def matmul(a, w, layer, out_dtype, n_cols=None, tm_pref=1024, tn_pref=1024):
    m, k = a.shape
    n = w.shape[2] if n_cols is None else n_cols
    tm, tn = _tile(m, tm_pref), _tile(n, tn_pref, LANES)
    osz = jnp.dtype(out_dtype).itemsize
    return pl.pallas_call(
        _mm_kernel,
        grid=(m // tm, n // tn),
        in_specs=[pl.BlockSpec((tm, k), lambda i, j: (i, 0)),
                  pl.BlockSpec((None, k, tn), lambda i, j: (layer, 0, j))],
        out_specs=pl.BlockSpec((tm, tn), lambda i, j: (i, j)),
        out_shape=jax.ShapeDtypeStruct((m, n), out_dtype),
        compiler_params=_cparams(("parallel", "parallel"),
                                 2 * (tm * k * 2 + k * tn * 2 + tm * tn * osz) + tm * tn * 4 + (4 << 20)),
        name="matmul",
    )(a, w)


def _mm_resid_kernel(*refs, n_a, k_sizes, tiles_per_row, fixed_row):
    a_refs = refs[:n_a]
    w_ref, r_ref, g_ref, o_ref = refs[n_a:]
    acc = None
    off = 0
    for a_ref, ks in zip(a_refs, k_sizes):
        part = _dot(a_ref[...], w_ref[off:off + ks, :])
        acc = part if acc is None else acc + part
        off += ks
    o_ref[...] = r_ref[...] + _mod_row(g_ref, tiles_per_row, fixed_row) * acc


def matmul_gated_residual(a_list, w, layer, resid, mods, gate_idx, rows_per_batch, fixed_row,
                          k_part=(0, 1), tm_pref=1024, tn_pref=1024):
    m, n = resid.shape
    part, n_parts = k_part
    assert n_parts == 1 or len(a_list) == 1
    k_sizes = tuple(a.shape[1] // n_parts for a in a_list)
    k = sum(k_sizes)
    tm = _tile(m if rows_per_batch is None else rows_per_batch, tm_pref)
    tn = _tile(n, tn_pref, LANES)
    tiles_per_row = None if rows_per_batch is None else rows_per_batch // tm
    nblk = n // tn
    return pl.pallas_call(
        functools.partial(_mm_resid_kernel, n_a=len(a_list), k_sizes=k_sizes,
                          tiles_per_row=tiles_per_row, fixed_row=fixed_row),
        grid=(m // tm, nblk),
        in_specs=[pl.BlockSpec((tm, ks), lambda i, j: (i, part)) for ks in k_sizes]
        + [pl.BlockSpec((None, k, tn), lambda i, j: (layer, part, j)),
           pl.BlockSpec((tm, tn), lambda i, j: (i, j)),
           pl.BlockSpec((ADA_ROWS, tn), lambda i, j: (0, gate_idx * nblk + j))],
        out_specs=pl.BlockSpec((tm, tn), lambda i, j: (i, j)),
        out_shape=jax.ShapeDtypeStruct((m, n), F32),
        compiler_params=_cparams(("parallel", "parallel"),
                                 2 * tm * k * 2 + 2 * k * tn * 2 + 5 * tm * tn * 4 + (4 << 20)),
        name="matmul_gated_residual",
    )(*a_list, w, resid, mods)


def _mm_swiglu_kernel(a_ref, wg_ref, wu_ref, o_ref):
    a = a_ref[...]
    g = _dot(a, wg_ref[...])
    u = _dot(a, wu_ref[...])
    o_ref[...] = (_silu(g) * u).astype(o_ref.dtype)


def matmul_swiglu(a, w, layer, tm_pref=2048, tn_pref=256):
    m, k = a.shape
    hdim = w.shape[2] // 2
    tm, tn = _tile(m, tm_pref), _tile(hdim, tn_pref, LANES)
    nblk = hdim // tn
    return pl.pallas_call(
        _mm_swiglu_kernel,
        grid=(m // tm, nblk),
        in_specs=[pl.BlockSpec((tm, k), lambda i, j: (i, 0)),
                  pl.BlockSpec((None, k, tn), lambda i, j: (layer, 0, j)),
                  pl.BlockSpec((None, k, tn), lambda i, j: (layer, 0, nblk + j))],
        out_specs=pl.BlockSpec((tm, tn), lambda i, j: (i, j)),
        out_shape=jax.ShapeDtypeStruct((m, hdim), BF16),
        compiler_params=_cparams(("parallel", "parallel"),
                                 2 * (tm * k * 2 + 2 * k * tn * 2 + tm * tn * 2) + 4 * tm * tn * 4 + (4 << 20)),
        name="matmul_swiglu",
    )(a, w, w)


def _gmlp_kernel(gu_ref, gv_ref, gn_ref, w_ref, b_ref, o_ref, *, group_w):
    v = _gelu_tanh(gv_ref[...])
    v = (v * lax.rsqrt(jnp.mean(v * v, axis=-1, keepdims=True) + EPS) * gn_ref[...]).astype(BF16)
    for n in range(gu_ref.shape[0] // CHUNK):
        rows = slice(n * CHUNK, (n + 1) * CHUNK)
        for g in range(GMLP_GROUPS):
            cols = slice(g * group_w, (g + 1) * group_w)
            s = _dot(w_ref[g], v[rows, cols]) + b_ref[:, cols]
            o_ref[rows, cols] = (_gelu_tanh(gu_ref[rows, cols]) * s).astype(o_ref.dtype)


def gmlp(proj, gu_blk, gv_blk, width, gn, w_s, layer, b_full):
    m = proj.shape[0]
    tr = _tile(m, 512, CHUNK)
    return pl.pallas_call(
        functools.partial(_gmlp_kernel, group_w=width // GMLP_GROUPS),
        grid=(m // tr,),
        in_specs=[pl.BlockSpec((tr, width), lambda i: (i, gu_blk)),
                  pl.BlockSpec((tr, width), lambda i: (i, gv_blk)),
                  pl.BlockSpec((1, width), lambda i: (0, 0)),
                  pl.BlockSpec((None, GMLP_GROUPS, CHUNK, CHUNK), lambda i: (layer, 0, 0, 0)),
                  pl.BlockSpec((CHUNK, width), lambda i: (0, 0))],
        out_specs=pl.BlockSpec((tr, width), lambda i: (i, 0)),
        out_shape=jax.ShapeDtypeStruct((m, width), BF16),
        compiler_params=_cparams(("parallel",), 12 * tr * width * 4),
        name="gmlp",
    )(proj, proj, gn.reshape(1, width), w_s, b_full)


def _conv_kernel(x_ref, prev_ref, next_ref, w_ref, b_ref, o_ref, *, tiles_per_seq):
    t = pl.program_id(0) % tiles_per_seq
    x = x_ref[...]
    rows = x.shape[0]
    rid = lax.broadcasted_iota(jnp.int32, x.shape, 0)
    prev_row = jnp.where(t == 0, 0.0, prev_ref[7:8, :])
    next_row = jnp.where(t == tiles_per_seq - 1, 0.0, next_ref[0:1, :])
    x_m1 = jnp.where(rid == 0, prev_row, pltpu.roll(x, 1, 0))
    x_p1 = jnp.where(rid == rows - 1, next_row, pltpu.roll(x, rows - 1, 0))
    y = w_ref[0:1, :] * x_m1 + w_ref[1:2, :] * x + w_ref[2:3, :] * x_p1 + b_ref[...]
    o_ref[...] = _silu(y)


def conv_silu(proj, col_blk, width, seq_len, w, b):
    m = proj.shape[0]
    tr = _tile(seq_len, 256)
    tiles_per_seq = seq_len // tr
    hb = tr // 8
    last = m // 8 - 1
    return pl.pallas_call(
        functools.partial(_conv_kernel, tiles_per_seq=tiles_per_seq),
        grid=(m // tr,),
        in_specs=[pl.BlockSpec((tr, width), lambda i: (i, col_blk)),
                  pl.BlockSpec((8, width), lambda i: (jnp.maximum(i * hb - 1, 0), col_blk)),
                  pl.BlockSpec((8, width), lambda i: (jnp.minimum((i + 1) * hb, last), col_blk)),
                  pl.BlockSpec((3, width), lambda i: (0, 0)),
                  pl.BlockSpec((1, width), lambda i: (0, 0))],
        out_specs=pl.BlockSpec((tr, width), lambda i: (i, 0)),
        out_shape=jax.ShapeDtypeStruct((m, width), F32),
        compiler_params=_cparams(("parallel",), 10 * tr * width * 4),
        name="conv_silu",
    )(proj, proj, proj, w, b.reshape(1, width))


def _ssd_kernel(*refs, reverse, dcol, fuse_out):
    if fuse_out:
        (x_ref, b_ref, c_ref, dt_ref, dtb_ref, alog_ref, sel_ref, h0_ref, yo_ref, z_ref, d_ref,
         gn_ref, y_ref, hfin_ref, h_scr) = refs
    else:
        (x_ref, b_ref, c_ref, dt_ref, dtb_ref, alog_ref, sel_ref, h0_ref,
         y_ref, hfin_ref, h_scr) = refs
    t = pl.program_id(1)
    q = CHUNK
    gw = SSM_HEADS_PER_GROUP * SSM_HEAD_DIM

    @pl.when(t == 0)
    def _():
        h_scr[...] = h0_ref[...]

    dt = _softplus(dt_ref[...] + dtb_ref[...])
    a = dt * (-jnp.exp(alog_ref[...]))
    ii = lax.broadcasted_iota(jnp.int32, (q, q), 0)
    jj = lax.broadcasted_iota(jnp.int32, (q, q), 1)
    tri = (jj >= ii) if reverse else (jj <= ii)
    tri_b = tri.astype(BF16)
    a_parts = _split_bf16(a, 3)
    ac = _dot(tri_b, a_parts[0]) + (_dot(tri_b, a_parts[1]) + _dot(tri_b, a_parts[2]))
    ac_t = ac.T
    edge = 0 if reverse else q - 1
    ac_end = ac[edge:edge + 1, :]

    def expand(mat):
        hi, lo = _split_bf16(mat, 2)
        return _dot(hi, sel_ref[...]) + _dot(lo, sel_ref[...])

    dt_x = expand(dt)
    grow_x = expand(jnp.exp(ac))
    wend_x = expand(dt * jnp.exp(ac_end - ac))
    lane = lax.broadcasted_iota(jnp.int32, (q, LANES), 1)

    for g in range(SSM_GROUPS):
        cols = slice(g * gw, (g + 1) * gw)
        xg = x_ref[:, cols]
        bg = b_ref[:, g * SSM_STATE:(g + 1) * SSM_STATE].astype(BF16)
        cg = c_ref[:, g * SSM_STATE:(g + 1) * SSM_STATE].astype(BF16)
        cb = lax.dot_general(cg, bg, (((1,), (1,)), ((), ())), preferred_element_type=F32)
        xdt = xg * dt_x[:, cols]
        h_prev = h_scr[g]
        y_parts = []
        for pair in range(SSM_HEADS_PER_GROUP // 2):
            ms = []
            for e in (2 * pair, 2 * pair + 1):
                c = dcol + g * SSM_HEADS_PER_GROUP + e
                seg = jnp.broadcast_to(ac[:, c:c + 1], (q, q)) - jnp.broadcast_to(ac_t[c:c + 1, :], (q, q))
                ms.append((cb * jnp.exp(jnp.where(tri, seg, -jnp.inf))).astype(BF16))
            xp = xdt[:, pair * LANES:(pair + 1) * LANES]
            rhs = jnp.concatenate([jnp.where(lane < SSM_HEAD_DIM, xp, 0.0),
                                   jnp.where(lane < SSM_HEAD_DIM, 0.0, xp)], axis=0).astype(BF16)
            y_parts.append(_dot(jnp.concatenate(ms, axis=1), rhs))
        y = jnp.concatenate(y_parts, axis=1) + _dot(cg, h_prev.astype(BF16)) * grow_x[:, cols]
        xdd = (xg * wend_x[:, cols]).astype(BF16)
        s_t = lax.dot_general(bg, xdd, (((0,), (0,)), ((), ())), preferred_element_type=F32)
        h_scr[g] = h_prev * grow_x[edge:edge + 1, cols] + s_t
        if fuse_out:
            y = (y + yo_ref[:, cols] + d_ref[:, cols] * xg) * _silu(z_ref[:, cols])
            y = y * lax.rsqrt(jnp.mean(y * y, axis=-1, keepdims=True) + EPS) * gn_ref[:, cols]
        y_ref[:, cols] = y.astype(y_ref.dtype)

    @pl.when(t == pl.num_programs(1) - 1)
    def _():
        hfin_ref[...] = h_scr[...]


def ssd_scan(xbc, dt_raw, dt_bias_row, a_log_row, h0, batch, seq_len, d_inner, reverse, dcol,
             out_stage=None):
    nchunks = seq_len // CHUNK
    gn = SSM_GROUPS * SSM_STATE
    gw = SSM_HEADS_PER_GROUP * SSM_HEAD_DIM
    bblk = d_inner // gn
    if reverse:
        row = lambda b, t: b * nchunks + (nchunks - 1 - t)
    else:
        row = lambda b, t: b * nchunks + t
    head_of_lane = dcol + jnp.arange(d_inner) // SSM_HEAD_DIM
    sel = (jnp.arange(LANES)[:, None] == head_of_lane[None, :]).astype(BF16)
    state_spec = pl.BlockSpec((None, SSM_GROUPS, SSM_STATE, gw), lambda b, t: (b, 0, 0, 0))
    wide = lambda c: pl.BlockSpec((CHUNK, d_inner), lambda b, t: (row(b, t), c))
    row_spec = pl.BlockSpec((1, LANES), lambda b, t: (0, 0))
    in_specs = [wide(0),
                pl.BlockSpec((CHUNK, gn), lambda b, t: (row(b, t), bblk)),
                pl.BlockSpec((CHUNK, gn), lambda b, t: (row(b, t), bblk + 1)),
                pl.BlockSpec((CHUNK, LANES), lambda b, t: (row(b, t), 0)),
                row_spec, row_spec,
                pl.BlockSpec((LANES, d_inner), lambda b, t: (0, 0)),
                state_spec]
    args = [xbc, xbc, xbc, dt_raw, dt_bias_row, a_log_row, sel, h0]
    if out_stage is not None:
        y_other, proj, z_blk, d_row, g_norm = out_stage
        full_row = pl.BlockSpec((1, d_inner), lambda b, t: (0, 0))
        in_specs += [wide(0), wide(z_blk), full_row, full_row]
        args += [y_other, proj, d_row, g_norm.reshape(1, d_inner)]
    return pl.pallas_call(
        functools.partial(_ssd_kernel, reverse=reverse, dcol=dcol, fuse_out=out_stage is not None),
        grid=(batch, nchunks),
        in_specs=in_specs,
        out_specs=[wide(0), state_spec],
        out_shape=[jax.ShapeDtypeStruct((batch * seq_len, d_inner), F32 if out_stage is None else BF16),
                   jax.ShapeDtypeStruct(h0.shape, F32)],
        scratch_shapes=[pltpu.VMEM((SSM_GROUPS, SSM_STATE, gw), F32)],
        compiler_params=_cparams(("parallel", "arbitrary"), 40 << 20),
        name="ssd_scan_bwd" if reverse else "ssd_scan_fwd",
    )(*args)


def _head_norm(t, g_row):
    lane = lax.broadcasted_iota(jnp.int32, t.shape, 1)
    lo = lane < ATTN_QK_DIM
    sq = t * t
    s_lo = jnp.sum(jnp.where(lo, sq, 0.0), axis=-1, keepdims=True)
    s_hi = jnp.sum(jnp.where(lo, 0.0, sq), axis=-1, keepdims=True)
    ms = jnp.where(lo, s_lo, s_hi) * (1.0 / ATTN_QK_DIM)
    return t * lax.rsqrt(ms + EPS) * g_row


def _rope(t, cos, sin_signed):
    lane = lax.broadcasted_iota(jnp.int32, t.shape, 1)
    swapped = jnp.where(lane % 2 == 0, pltpu.roll(t, LANES - 1, 1), pltpu.roll(t, 1, 1))
    return t * cos + swapped * sin_signed


def _qkv_prep_kernel(*refs, use_rope):
    if use_rope:
        q_ref, k_ref, v_ref, gq_ref, gk_ref, cos_ref, sin_ref, qo_ref, ko_ref, vo_ref = refs
    else:
        q_ref, k_ref, v_ref, gq_ref, gk_ref, qo_ref, ko_ref, vo_ref = refs
    vo_ref[...] = v_ref[...].astype(vo_ref.dtype)
    for h in range(ATTN_HEADS):
        cols = slice(h * LANES, (h + 1) * LANES)
        qh = _head_norm(q_ref[:, cols], gq_ref[...])
        kh = _head_norm(k_ref[:, cols], gk_ref[...])
        if use_rope:
            qh = _rope(qh, cos_ref[...], sin_ref[...])
            kh = _rope(kh, cos_ref[...], sin_ref[...])
        qo_ref[:, cols] = (qh * Q_SCALE).astype(qo_ref.dtype)
        ko_ref[cols, :] = kh.T.astype(ko_ref.dtype)


def qkv_prep(proj, q_blk, width, seq_len, gq, gk, rope):
    m = proj.shape[0]
    tr = _tile(seq_len, 256, LANES)
    tiles_per_seq = seq_len // tr
    blk = lambda c: pl.BlockSpec((tr, width), lambda i: (i, c))
    row = pl.BlockSpec((1, LANES), lambda i: (0, 0))
    in_specs = [blk(q_blk), blk(q_blk + 1), blk(q_blk + 2), row, row]
    args = [proj, proj, proj, jnp.tile(gq, 2).reshape(1, LANES), jnp.tile(gk, 2).reshape(1, LANES)]
    if rope is not None:
        tab = pl.BlockSpec((tr, LANES), lambda i: (i % tiles_per_seq, 0))
        in_specs += [tab, tab]
        args += list(rope)
    out = jax.ShapeDtypeStruct((m, width), BF16)
    kt_spec = pl.BlockSpec((None, width, tr), lambda i: (i // tiles_per_seq, 0, i % tiles_per_seq))
    return pl.pallas_call(
        functools.partial(_qkv_prep_kernel, use_rope=rope is not None),
        grid=(m // tr,),
        in_specs=in_specs,
        out_specs=[blk(0), kt_spec, blk(0)],
        out_shape=[out, jax.ShapeDtypeStruct((m // seq_len, width, seq_len), BF16), out],
        compiler_params=_cparams(("parallel",), 16 * tr * width * 4),
        name="qkv_prep",
    )(*args)


def _attn_kernel(*refs, n_seg, seg_lens, tq, n_blk, key_blk, lam_init):
    q_ref = refs[0]
    kv_refs = refs[1:1 + 2 * n_seg]
    lq_ref, lk_ref, g_ref, o_ref, s_a, s_b, p_a, p_b = refs[1 + 2 * n_seg:]
    lam = (jnp.exp(jnp.sum(lq_ref[0:1, :] * lk_ref[0:1, :], keepdims=True))
           - jnp.exp(jnp.sum(lq_ref[1:2, :] * lk_ref[1:2, :], keepdims=True)) + lam_init)
    lane = lax.broadcasted_iota(jnp.int32, (tq, LANES), 1)
    blocks = []
    off = 0
    for i in range(n_seg):
        blocks += [(i, c0, off + c0) for c0 in range(0, seg_lens[i], key_blk)]
        off += seg_lens[i]
    lane_blocks = range(0, key_blk, LANES)
    n_keys = sum(seg_lens)
    sm_rows = min(tq, 64)

    def score_items(q, s_buf):
        items = []
        for m_idx in range(2):
            in_map = (lane < ATTN_QK_DIM) if m_idx == 0 else (lane >= ATTN_QK_DIM)
            qm = jnp.where(in_map, q, jnp.zeros_like(q))

            def item(qm=qm, m_idx=m_idx, i=0, c0=0, g0=0):
                s_buf[m_idx, :, g0:g0 + key_blk] = _dot(qm, kv_refs[2 * i][:, c0:c0 + key_blk])

            items += [functools.partial(item, i=i, c0=c0, g0=g0) for i, c0, g0 in blocks]
        return items

    def interleave(vpu_items, mxu_items):
        done = 0
        for n, f in enumerate(vpu_items):
            f()
            upto = (n + 1) * len(mxu_items) // len(vpu_items)
            for g in mxu_items[done:upto]:
                g()
            done = upto

    def phase(s_buf, p_buf, next_items):
        denoms = {}
        lane_cols = [slice(c, c + LANES) for c in range(0, n_keys, LANES)]

        def softmax_item(m_idx, r0):
            rows = slice(r0, r0 + sm_rows)
            mx = functools.reduce(jnp.maximum, [s_buf[m_idx, rows, cols] for cols in lane_cols])
            mxb = jnp.broadcast_to(jnp.max(mx, axis=-1, keepdims=True), (sm_rows, LANES))
            acc = None
            for cols in lane_cols:
                p = jnp.exp2(s_buf[m_idx, rows, cols] - mxb)
                acc = p if acc is None else acc + p
                p_buf[m_idx, rows, cols] = p.astype(BF16)
            denoms[m_idx, r0] = jnp.sum(acc, axis=-1, keepdims=True)

        row_starts = range(0, tq, sm_rows)
        interleave([functools.partial(softmax_item, m_idx, r0) for r0 in row_starts for m_idx in range(2)],
                   next_items)
        denom = [jnp.concatenate([denoms[m_idx, r0] for r0 in row_starts], axis=0) for m_idx in range(2)]
        r1 = jnp.broadcast_to(1.0 / denom[0], (tq, LANES)).astype(BF16)
        r2 = jnp.broadcast_to(lam / denom[1], (tq, LANES)).astype(BF16)
        o = None
        for i, c0, g0 in blocks:
            ws = [p_buf[0, :, g0 + c:g0 + c + LANES] * r1 - p_buf[1, :, g0 + c:g0 + c + LANES] * r2
                  for c in lane_blocks]
            part = _dot(jnp.concatenate(ws, axis=1), kv_refs[2 * i + 1][c0:c0 + key_blk, :])
            o = part if o is None else o + part
        o = o * lax.rsqrt(jnp.mean(o * o, axis=-1, keepdims=True) + EPS) * g_ref[...]
        return (o * (1.0 - lam_init)).astype(o_ref.dtype)

    bufs = [(s_a, p_a), (s_b, p_b)]
    for f in score_items(q_ref[0:tq, :], s_a):
        f()
    for j in range(n_blk):
        s_cur, p_cur = bufs[j % 2]
        nxt = score_items(q_ref[(j + 1) * tq:(j + 2) * tq, :], bufs[(j + 1) % 2][0]) if j + 1 < n_blk else []
        o_ref[j * tq:(j + 1) * tq, :] = phase(s_cur, p_cur, nxt)


def diff_attention(q, kv_segments, batch, lam_q, lam_k, subln_g, lam_init):
    mq, width = q.shape
    lq = mq // batch
    tq = _tile(lq // 2, 256, 16)
    n_blk = min(lq // tq, 4)
    steps = lq // (n_blk * tq)
    assert steps * n_blk * tq == lq
    seg_lens = tuple(s for _, _, s in kv_segments)
    n_keys = sum(seg_lens)
    key_blk = 256 if all(s % 256 == 0 for s in seg_lens) else LANES
    q_spec = pl.BlockSpec((n_blk * tq, LANES), lambda b, h, i: (b * steps + i, h))
    in_specs = [q_spec]
    args = [q]
    for kt, v, seg_len in kv_segments:
        in_specs += [pl.BlockSpec((None, LANES, seg_len), lambda b, h, i: (b, h, 0)),
                     pl.BlockSpec((seg_len, LANES), lambda b, h, i: (b, h))]
        args += [kt, v]
    small = pl.BlockSpec((2, ATTN_QK_DIM), lambda b, h, i: (0, 0))
    in_specs += [small, small, pl.BlockSpec((1, LANES), lambda b, h, i: (0, 0))]
    args += [lam_q, lam_k, subln_g.reshape(1, LANES)]
    return pl.pallas_call(
        functools.partial(_attn_kernel, n_seg=len(kv_segments), seg_lens=seg_lens, tq=tq,
                          n_blk=n_blk, key_blk=key_blk, lam_init=lam_init),
        grid=(batch, ATTN_HEADS, steps),
        in_specs=in_specs,
        out_specs=q_spec,
        out_shape=jax.ShapeDtypeStruct((mq, width), BF16),
        scratch_shapes=[pltpu.VMEM((2, tq, n_keys), F32), pltpu.VMEM((2, tq, n_keys), F32),
                        pltpu.VMEM((2, tq, n_keys), BF16), pltpu.VMEM((2, tq, n_keys), BF16)],
        compiler_params=_cparams(("parallel", "parallel", "arbitrary"),
                                 8 * n_keys * LANES * 2 + 2 * tq * n_keys * (12 + 12) + (6 << 20)),
        name="diff_attention",
    )(*args)


def _rope_tables(seq_len):
    pos = jnp.arange(seq_len)
    row, col = pos // GRID_W, pos % GRID_W
    n_freq = ATTN_QK_DIM // 4
    inv = ROPE_BASE ** (-jnp.arange(n_freq, dtype=F32) / n_freq)
    ang = jnp.concatenate([row[:, None] * inv, col[:, None] * inv], axis=-1)
    cos = jnp.repeat(jnp.cos(ang), 2, axis=-1)
    sin = jnp.repeat(jnp.sin(ang), 2, axis=-1) * jnp.tile(jnp.array([-1.0, 1.0], F32), ATTN_QK_DIM // 2)
    return jnp.tile(cos, (1, 2)), jnp.tile(sin, (1, 2))


def _pad_lanes(v):
    return jnp.pad(v.reshape(1, -1), ((0, 0), (0, LANES - v.size)))


def kernel(x, c, ctx, c_ctx, ada_down, ada_up, ada_bias, norm1_g, w_in, gm_norm_g, gm_w, gm_b,
           conv_w, conv_b, dt_bias, a_log, d_skip, ssm_norm_g, q_norm_g, k_norm_g, lam_q, lam_k,
           subln_g, w_out, norm2_g, w_ffn_in, w_ffn_out):
    batch, seq, d = x.shape
    lc = ctx.shape[1]
    depth = w_in.shape[0]
    w_gmlp, w_ssm, w_attn = d // 4, d // 2, d // 4
    n_heads = w_ssm // SSM_HEAD_DIM
    gn = SSM_GROUPS * SSM_STATE
    conv_ch = w_ssm + 2 * gn
    assert batch < ADA_ROWS and 2 * n_heads <= LANES
    assert n_heads == SSM_GROUPS * SSM_HEADS_PER_GROUP and w_attn == ATTN_HEADS * LANES
    assert seq % CHUNK == 0 and lc % CHUNK == 0 and seq % GRID_W == 0
    off_dt = 2 * w_gmlp + w_ssm + conv_ch
    off_q = off_dt + 2 * n_heads

    xs = x.reshape(batch * seq, d)
    ss = ctx.reshape(batch * lc, d)
    cc = jnp.zeros((ADA_ROWS, d), F32).at[:batch].set(c).at[batch].set(c_ctx)
    rope = _rope_tables(seq)
    h_zero = jnp.zeros((batch, SSM_GROUPS, SSM_STATE, SSM_HEADS_PER_GROUP * SSM_HEAD_DIM), F32)
    lat = dict(rows_per_batch=seq, fixed_row=None)
    con = dict(rows_per_batch=None, fixed_row=batch)

    w_in_b = w_in.astype(BF16)
    w_qkv = w_in_b[:, :, off_q:]
    w_dt = jnp.pad(w_in_b[:, :, off_dt:off_q], ((0, 0), (0, 0), (0, LANES - 2 * n_heads)))
    w_o = w_out.astype(BF16)
    w_f1 = w_ffn_in.astype(BF16)
    w_f2 = w_ffn_out.astype(BF16)
    gm_w_b = gm_w.astype(BF16)
    z_blk = 2 * w_gmlp // w_ssm
    xbc_blk = (2 * w_gmlp + w_ssm) // conv_ch

    for l in range(depth):
        last = l == depth - 1
        lam_init = 0.8 - 0.6 * math.exp(-0.3 * l)
        gm_b_full = jnp.repeat(gm_b[l].T, w_gmlp // GMLP_GROUPS, axis=1)
        d_row = jnp.repeat(d_skip[l], SSM_HEAD_DIM).reshape(1, w_ssm)
        dtb_row = _pad_lanes(dt_bias[l])
        alog_row = _pad_lanes(a_log[l])

        mods = adaln(cc, ada_down[l], ada_up[l], ada_bias[l])

        def mixers_in(stream, seq_len, sel, use_rope):
            h = modnorm(stream, norm1_g[l], mods, 0, 1, **sel)
            proj = matmul(h, w_in_b, l, F32, n_cols=off_dt)
            pqkv = matmul(h, w_qkv, l, F32)
            dt_raw = matmul(h, w_dt, l, F32)
            xbc = conv_silu(proj, xbc_blk, conv_ch, seq_len, conv_w[l], conv_b[l])
            q, k, v = qkv_prep(pqkv, 0, w_attn, seq_len, q_norm_g[l], k_norm_g[l],
                               rope if use_rope else None)
            return proj, dt_raw, xbc, q, k, v

        def ssm(proj, dt_raw, xbc, seq_len, h0_f, h0_b):
            y_f, h_f = ssd_scan(xbc, dt_raw, dtb_row, alog_row, h0_f, batch, seq_len, w_ssm, False, 0)
            y, h_b = ssd_scan(xbc, dt_raw, dtb_row, alog_row, h0_b, batch, seq_len, w_ssm, True, n_heads,
                              out_stage=(y_f, proj, z_blk, d_row, ssm_norm_g[l]))
            return y, h_f, h_b

        def mixers_out(stream, a, y, o, sel):
            stream = matmul_gated_residual([a, y, o], w_o, l, stream, mods, 2, **sel)
            h = modnorm(stream, norm2_g[l], mods, 3, 4, **sel)
            act = matmul_swiglu(h, w_f1, l)
            for part in range(2):
                stream = matmul_gated_residual([act], w_f2, l, stream, mods, 5, k_part=(part, 2),
                                               tn_pref=512, **sel)
            return stream

        proj_c, dt_c, xbc_c, q_c, k_c, v_c = mixers_in(ss, lc, con, False)
        proj_l, dt_l, xbc_l, q_l, k_l, v_l = mixers_in(xs, seq, lat, True)

        y_c, hf_c, hb_c = ssm(proj_c, dt_c, xbc_c, lc, h_zero, h_zero)
        y_l, _, _ = ssm(proj_l, dt_l, xbc_l, seq, hf_c, hb_c)

        o_l = diff_attention(q_l, [(k_c, v_c, lc), (k_l, v_l, seq)], batch, lam_q[l], lam_k[l],
                             subln_g[l], lam_init)
        a_l = gmlp(proj_l, 0, 1, w_gmlp, gm_norm_g[l], gm_w_b, l, gm_b_full)
        xs = mixers_out(xs, a_l, y_l, o_l, lat)

        if not last:
            o_c = diff_attention(q_c, [(k_c, v_c, lc)], batch, lam_q[l], lam_k[l], subln_g[l], lam_init)
            a_c = gmlp(proj_c, 0, 1, w_gmlp, gm_norm_g[l], gm_w_b, l, gm_b_full)
            ss = mixers_out(ss, a_c, y_c, o_c, con)

    return xs.reshape(batch, seq, d)
```

```python
import functools
import math

import jax
import jax.numpy as jnp
from jax import lax
from jax.experimental import pallas as pl
from jax.experimental.pallas import tpu as pltpu

F32 = jnp.float32
BF16 = jnp.bfloat16

GRID_W = 64
CHUNK = 128
GMLP_GROUPS = 8
SSM_HEAD_DIM = 64
SSM_GROUPS = 8
SSM_STATE = 128
SSM_HEADS_PER_GROUP = 4
ATTN_HEADS = 8
ATTN_QK_DIM = 64
ROPE_BASE = 10000.0
EPS = 1e-6
ADA_ROWS = 8
Q_SCALE = ATTN_QK_DIM ** -0.5 * math.log2(math.e)

LANES = 128
VMEM_CAP = 56 * 1024 * 1024


def _tile(n, pref, mult=8):
    if n <= pref:
        return n
    t = (pref // mult) * mult
    while t > mult and n % t:
        t -= mult
    assert n % t == 0, (n, pref, mult)
    return t


def _cparams(sem, vmem_bytes):
    return pltpu.CompilerParams(dimension_semantics=sem,
                                vmem_limit_bytes=int(min(max(vmem_bytes, 16 << 20), VMEM_CAP)))


def _dot(a, b):
    return jnp.dot(a, b, preferred_element_type=F32)


def _split_bf16(x, terms):
    parts = []
    for _ in range(terms):
        p = x.astype(BF16)
        parts.append(p)
        x = x - p.astype(F32)
    return parts


def _dot_x3(a, b):
    ah, al = _split_bf16(a, 2)
    bh, bl = _split_bf16(b, 2)
    return _dot(ah, bh) + (_dot(ah, bl) + _dot(al, bh))


def _silu(x):
    return x * jax.nn.sigmoid(x)


def _gelu_tanh(x):
    return 0.5 * x * (1.0 + jnp.tanh(math.sqrt(2.0 / math.pi) * (x + 0.044715 * (x * x * x))))


def _softplus(x):
    return jnp.maximum(x, 0.0) + jnp.log1p(jnp.exp(-jnp.abs(x)))


def _mod_row(ref, tiles_per_row, fixed_row, axis=0):
    if tiles_per_row is None:
        return ref[fixed_row:fixed_row + 1, :]
    return ref[pl.ds(pl.program_id(axis) // tiles_per_row, 1), :]


def _adaln_kernel(cc_ref, down_ref, up_ref, bias_ref, o_ref, h_ref):
    @pl.when(pl.program_id(0) == 0)
    def _():
        h_ref[...] = _dot_x3(_silu(cc_ref[...]), down_ref[...])

    o_ref[...] = _dot_x3(h_ref[...], up_ref[...]) + bias_ref[...]


def adaln(cc, down, up, bias):
    d, r = down.shape
    n = up.shape[1]
    tn = _tile(n, 2048, LANES)
    return pl.pallas_call(
        _adaln_kernel,
        grid=(n // tn,),
        in_specs=[pl.BlockSpec((ADA_ROWS, d), lambda j: (0, 0)),
                  pl.BlockSpec((d, r), lambda j: (0, 0)),
                  pl.BlockSpec((r, tn), lambda j: (0, j)),
                  pl.BlockSpec((1, tn), lambda j: (0, j))],
        out_specs=pl.BlockSpec((ADA_ROWS, tn), lambda j: (0, j)),
        out_shape=jax.ShapeDtypeStruct((ADA_ROWS, n), F32),
        scratch_shapes=[pltpu.VMEM((ADA_ROWS, r), F32)],
        compiler_params=_cparams(("arbitrary",), 6 * d * r * 4 + 8 * r * tn * 4),
        name="adaln",
    )(cc, down, up, bias.reshape(1, n))


def _modnorm_kernel(x_ref, g_ref, sh_ref, sc_ref, o_ref, *, tiles_per_row, fixed_row):
    x = x_ref[...]
    y = x * lax.rsqrt(jnp.mean(x * x, axis=-1, keepdims=True) + EPS) * g_ref[...]
    sh = _mod_row(sh_ref, tiles_per_row, fixed_row)
    sc = _mod_row(sc_ref, tiles_per_row, fixed_row)
    o_ref[...] = (y * (1.0 + sc) + sh).astype(o_ref.dtype)


def modnorm(x, g, mods, shift_idx, scale_idx, rows_per_batch, fixed_row):
    m, d = x.shape
    tr = _tile(m if rows_per_batch is None else rows_per_batch, 512)
    tiles_per_row = None if rows_per_batch is None else rows_per_batch // tr
    return pl.pallas_call(
        functools.partial(_modnorm_kernel, tiles_per_row=tiles_per_row, fixed_row=fixed_row),
        grid=(m // tr,),
        in_specs=[pl.BlockSpec((tr, d), lambda i: (i, 0)),
                  pl.BlockSpec((1, d), lambda i: (0, 0)),
                  pl.BlockSpec((ADA_ROWS, d), lambda i: (0, shift_idx)),
                  pl.BlockSpec((ADA_ROWS, d), lambda i: (0, scale_idx))],
        out_specs=pl.BlockSpec((tr, d), lambda i: (i, 0)),
        out_shape=jax.ShapeDtypeStruct((m, d), BF16),
        compiler_params=_cparams(("parallel",), 8 * tr * d * 4),
        name="modnorm",
    )(x, g.reshape(1, d), mods, mods)


def _mm_kernel(a_ref, w_ref, o_ref):
    o_ref[...] = _dot(a_ref[...], w_ref[...]).astype(o_ref.dtype)


def matmul(a, w, layer, out_dtype, n_cols=None, tm_pref=1024, tn_pref=1024):
    m, k = a.shape
    n = w.shape[2] if n_cols is None else n_cols
    tm, tn = _tile(m, tm_pref), _tile(n, tn_pref, LANES)
    osz = jnp.dtype(out_dtype).itemsize
    return pl.pallas_call(
        _mm_kernel,
        grid=(m // tm, n // tn),
        in_specs=[pl.BlockSpec((tm, k), lambda i, j: (i, 0)),
                  pl.BlockSpec((None, k, tn), lambda i, j: (layer, 0, j))],
        out_specs=pl.BlockSpec((tm, tn), lambda i, j: (i, j)),
        out_shape=jax.ShapeDtypeStruct((m, n), out_dtype),
        compiler_params=_cparams(("parallel", "parallel"),
                                 2 * (tm * k * 2 + k * tn * 2 + tm * tn * osz) + tm * tn * 4 + (4 << 20)),
        name="matmul",
    )(a, w)


def _mm_resid_kernel(*refs, n_a, k_sizes, tiles_per_row, fixed_row):
    a_refs = refs[:n_a]
    w_ref, r_ref, g_ref, o_ref = refs[n_a:]
    acc = None
    off = 0
    for a_ref, ks in zip(a_refs, k_sizes):
        part = _dot(a_ref[...], w_ref[off:off + ks, :])
        acc = part if acc is None else acc + part
        off += ks
    o_ref[...] = r_ref[...] + _mod_row(g_ref, tiles_per_row, fixed_row) * acc


def matmul_gated_residual(a_list, w, layer, resid, mods, gate_idx, rows_per_batch, fixed_row,
                          k_part=(0, 1), tm_pref=1024, tn_pref=1024):
    m, n = resid.shape
    part, n_parts = k_part
    assert n_parts == 1 or len(a_list) == 1
    k_sizes = tuple(a.shape[1] // n_parts for a in a_list)
    k = sum(k_sizes)
    tm = _tile(m if rows_per_batch is None else rows_per_batch, tm_pref)
    tn = _tile(n, tn_pref, LANES)
    tiles_per_row = None if rows_per_batch is None else rows_per_batch // tm
    nblk = n // tn
    return pl.pallas_call(
        functools.partial(_mm_resid_kernel, n_a=len(a_list), k_sizes=k_sizes,
                          tiles_per_row=tiles_per_row, fixed_row=fixed_row),
        grid=(m // tm, nblk),
        in_specs=[pl.BlockSpec((tm, ks), lambda i, j: (i, part)) for ks in k_sizes]
        + [pl.BlockSpec((None, k, tn), lambda i, j: (layer, part, j)),
           pl.BlockSpec((tm, tn), lambda i, j: (i, j)),
           pl.BlockSpec((ADA_ROWS, tn), lambda i, j: (0, gate_idx * nblk + j))],
        out_specs=pl.BlockSpec((tm, tn), lambda i, j: (i, j)),
        out_shape=jax.ShapeDtypeStruct((m, n), F32),
        compiler_params=_cparams(("parallel", "parallel"),
                                 2 * tm * k * 2 + 2 * k * tn * 2 + 5 * tm * tn * 4 + (4 << 20)),
        name="matmul_gated_residual",
    )(*a_list, w, resid, mods)


def _mm_swiglu_kernel(a_ref, wg_ref, wu_ref, o_ref):
    a = a_ref[...]
    g = _dot(a, wg_ref[...])
    u = _dot(a, wu_ref[...])
    o_ref[...] = (_silu(g) * u).astype(o_ref.dtype)


def matmul_swiglu(a, w, layer, tm_pref=2048, tn_pref=256):
    m, k = a.shape
    hdim = w.shape[2] // 2
    tm, tn = _tile(m, tm_pref), _tile(hdim, tn_pref, LANES)
    nblk = hdim // tn
    return pl.pallas_call(
        _mm_swiglu_kernel,
        grid=(m // tm, nblk),
        in_specs=[pl.BlockSpec((tm, k), lambda i, j: (i, 0)),
                  pl.BlockSpec((None, k, tn), lambda i, j: (layer, 0, j)),
                  pl.BlockSpec((None, k, tn), lambda i, j: (layer, 0, nblk + j))],
        out_specs=pl.BlockSpec((tm, tn), lambda i, j: (i, j)),
        out_shape=jax.ShapeDtypeStruct((m, hdim), BF16),
        compiler_params=_cparams(("parallel", "parallel"),
                                 2 * (tm * k * 2 + 2 * k * tn * 2 + tm * tn * 2) + 4 * tm * tn * 4 + (4 << 20)),
        name="matmul_swiglu",
    )(a, w, w)


def _gmlp_kernel(gu_ref, gv_ref, gn_ref, w_ref, b_ref, o_ref, *, group_w):
    v = _gelu_tanh(gv_ref[...])
    v = (v * lax.rsqrt(jnp.mean(v * v, axis=-1, keepdims=True) + EPS) * gn_ref[...]).astype(BF16)
    for n in range(gu_ref.shape[0] // CHUNK):
        rows = slice(n * CHUNK, (n + 1) * CHUNK)
        for g in range(GMLP_GROUPS):
            cols = slice(g * group_w, (g + 1) * group_w)
            s = _dot(w_ref[g], v[rows, cols]) + b_ref[:, cols]
            o_ref[rows, cols] = (_gelu_tanh(gu_ref[rows, cols]) * s).astype(o_ref.dtype)


def gmlp(proj, gu_blk, gv_blk, width, gn, w_s, layer, b_full):
    m = proj.shape[0]
    tr = _tile(m, 512, CHUNK)
    return pl.pallas_call(
        functools.partial(_gmlp_kernel, group_w=width // GMLP_GROUPS),
        grid=(m // tr,),
        in_specs=[pl.BlockSpec((tr, width), lambda i: (i, gu_blk)),
                  pl.BlockSpec((tr, width), lambda i: (i, gv_blk)),
                  pl.BlockSpec((1, width), lambda i: (0, 0)),
                  pl.BlockSpec((None, GMLP_GROUPS, CHUNK, CHUNK), lambda i: (layer, 0, 0, 0)),
                  pl.BlockSpec((CHUNK, width), lambda i: (0, 0))],
        out_specs=pl.BlockSpec((tr, width), lambda i: (i, 0)),
        out_shape=jax.ShapeDtypeStruct((m, width), BF16),
        compiler_params=_cparams(("parallel",), 12 * tr * width * 4),
        name="gmlp",
    )(proj, proj, gn.reshape(1, width), w_s, b_full)


def _conv_kernel(x_ref, prev_ref, next_ref, w_ref, b_ref, o_ref, *, tiles_per_seq):
    t = pl.program_id(0) % tiles_per_seq
    x = x_ref[...]
    rows = x.shape[0]
    rid = lax.broadcasted_iota(jnp.int32, x.shape, 0)
    prev_row = jnp.where(t == 0, 0.0, prev_ref[7:8, :])
    next_row = jnp.where(t == tiles_per_seq - 1, 0.0, next_ref[0:1, :])
    x_m1 = jnp.where(rid == 0, prev_row, pltpu.roll(x, 1, 0))
    x_p1 = jnp.where(rid == rows - 1, next_row, pltpu.roll(x, rows - 1, 0))
    y = w_ref[0:1, :] * x_m1 + w_ref[1:2, :] * x + w_ref[2:3, :] * x_p1 + b_ref[...]
    o_ref[...] = _silu(y)


def conv_silu(proj, col_blk, width, seq_len, w, b):
    m = proj.shape[0]
    tr = _tile(seq_len, 256)
    tiles_per_seq = seq_len // tr
    hb = tr // 8
    last = m // 8 - 1
    return pl.pallas_call(
        functools.partial(_conv_kernel, tiles_per_seq=tiles_per_seq),
        grid=(m // tr,),
        in_specs=[pl.BlockSpec((tr, width), lambda i: (i, col_blk)),
                  pl.BlockSpec((8, width), lambda i: (jnp.maximum(i * hb - 1, 0), col_blk)),
                  pl.BlockSpec((8, width), lambda i: (jnp.minimum((i + 1) * hb, last), col_blk)),
                  pl.BlockSpec((3, width), lambda i: (0, 0)),
                  pl.BlockSpec((1, width), lambda i: (0, 0))],
        out_specs=pl.BlockSpec((tr, width), lambda i: (i, 0)),
        out_shape=jax.ShapeDtypeStruct((m, width), F32),
        compiler_params=_cparams(("parallel",), 10 * tr * width * 4),
        name="conv_silu",
    )(proj, proj, proj, w, b.reshape(1, width))


def _ssd_kernel(*refs, reverse, dcol, fuse_out):
    if fuse_out:
        (x_ref, b_ref, c_ref, dt_ref, dtb_ref, alog_ref, sel_ref, h0_ref, yo_ref, z_ref, d_ref,
         gn_ref, y_ref, hfin_ref, h_scr) = refs
    else:
        (x_ref, b_ref, c_ref, dt_ref, dtb_ref, alog_ref, sel_ref, h0_ref,
         y_ref, hfin_ref, h_scr) = refs
    t = pl.program_id(1)
    q = CHUNK
    gw = SSM_HEADS_PER_GROUP * SSM_HEAD_DIM

    @pl.when(t == 0)
    def _():
        h_scr[...] = h0_ref[...]

    dt = _softplus(dt_ref[...] + dtb_ref[...])
    a = dt * (-jnp.exp(alog_ref[...]))
    ii = lax.broadcasted_iota(jnp.int32, (q, q), 0)
    jj = lax.broadcasted_iota(jnp.int32, (q, q), 1)
    tri = (jj >= ii) if reverse else (jj <= ii)
    tri_b = tri.astype(BF16)
    a_parts = _split_bf16(a, 3)
    ac = _dot(tri_b, a_parts[0]) + (_dot(tri_b, a_parts[1]) + _dot(tri_b, a_parts[2]))
    ac_t = ac.T
    edge = 0 if reverse else q - 1
    ac_end = ac[edge:edge + 1, :]

    def expand(mat):
        hi, lo = _split_bf16(mat, 2)
        return _dot(hi, sel_ref[...]) + _dot(lo, sel_ref[...])

    dt_x = expand(dt)
    grow_x = expand(jnp.exp(ac))
    wend_x = expand(dt * jnp.exp(ac_end - ac))
    lane = lax.broadcasted_iota(jnp.int32, (q, LANES), 1)

    groups = range(SSM_GROUPS)
    cols = [slice(g * gw, (g + 1) * gw) for g in groups]
    ncols = [slice(g * SSM_STATE, (g + 1) * SSM_STATE) for g in groups]
    bgs = [b_ref[:, ncols[g]].astype(BF16) for g in groups]
    cgs = [c_ref[:, ncols[g]].astype(BF16) for g in groups]
    cbs = [lax.dot_general(cgs[g], bgs[g], (((1,), (1,)), ((), ())), preferred_element_type=F32)
           for g in groups]
    y_offs = [_dot(cgs[g], h_scr[g].astype(BF16)) for g in groups]
    s_ts = [lax.dot_general(bgs[g], (x_ref[:, cols[g]] * wend_x[:, cols[g]]).astype(BF16),
                            (((0,), (0,)), ((), ())), preferred_element_type=F32) for g in groups]
    for g in groups:
        h_scr[g] = h_scr[g] * grow_x[edge:edge + 1, cols[g]] + s_ts[g]
    for g in groups:
        xdt = x_ref[:, cols[g]] * dt_x[:, cols[g]]
        y_parts = []
        for pair in range(SSM_HEADS_PER_GROUP // 2):
            ms = []
            for e in (2 * pair, 2 * pair + 1):
                c = dcol + g * SSM_HEADS_PER_GROUP + e
                seg = jnp.broadcast_to(ac[:, c:c + 1], (q, q)) - jnp.broadcast_to(ac_t[c:c + 1, :], (q, q))
                ms.append((cbs[g] * jnp.exp(jnp.where(tri, seg, -jnp.inf))).astype(BF16))
            xp = xdt[:, pair * LANES:(pair + 1) * LANES]
            rhs = jnp.concatenate([jnp.where(lane < SSM_HEAD_DIM, xp, 0.0),
                                   jnp.where(lane < SSM_HEAD_DIM, 0.0, xp)], axis=0).astype(BF16)
            y_parts.append(_dot(jnp.concatenate(ms, axis=1), rhs))
        y = jnp.concatenate(y_parts, axis=1) + y_offs[g] * grow_x[:, cols[g]]
        if fuse_out:
            y = (y + yo_ref[:, cols[g]] + d_ref[:, cols[g]] * x_ref[:, cols[g]]) * _silu(z_ref[:, cols[g]])
            y = y * lax.rsqrt(jnp.mean(y * y, axis=-1, keepdims=True) + EPS) * gn_ref[:, cols[g]]
        y_ref[:, cols[g]] = y.astype(y_ref.dtype)

    @pl.when(t == pl.num_programs(1) - 1)
    def _():
        hfin_ref[...] = h_scr[...]


def ssd_scan(xbc, dt_raw, dt_bias_row, a_log_row, h0, batch, seq_len, d_inner, reverse, dcol,
             out_stage=None):
    nchunks = seq_len // CHUNK
    gn = SSM_GROUPS * SSM_STATE
    gw = SSM_HEADS_PER_GROUP * SSM_HEAD_DIM
    bblk = d_inner // gn
    if reverse:
        row = lambda b, t: b * nchunks + (nchunks - 1 - t)
    else:
        row = lambda b, t: b * nchunks + t
    head_of_lane = dcol + jnp.arange(d_inner) // SSM_HEAD_DIM
    sel = (jnp.arange(LANES)[:, None] == head_of_lane[None, :]).astype(BF16)
    state_spec = pl.BlockSpec((None, SSM_GROUPS, SSM_STATE, gw), lambda b, t: (b, 0, 0, 0))
    wide = lambda c: pl.BlockSpec((CHUNK, d_inner), lambda b, t: (row(b, t), c))
    row_spec = pl.BlockSpec((1, LANES), lambda b, t: (0, 0))
    in_specs = [wide(0),
                pl.BlockSpec((CHUNK, gn), lambda b, t: (row(b, t), bblk)),
                pl.BlockSpec((CHUNK, gn), lambda b, t: (row(b, t), bblk + 1)),
                pl.BlockSpec((CHUNK, LANES), lambda b, t: (row(b, t), 0)),
                row_spec, row_spec,
                pl.BlockSpec((LANES, d_inner), lambda b, t: (0, 0)),
                state_spec]
    args = [xbc, xbc, xbc, dt_raw, dt_bias_row, a_log_row, sel, h0]
    if out_stage is not None:
        y_other, proj, z_blk, d_row, g_norm = out_stage
        full_row = pl.BlockSpec((1, d_inner), lambda b, t: (0, 0))
        in_specs += [wide(0), wide(z_blk), full_row, full_row]
        args += [y_other, proj, d_row, g_norm.reshape(1, d_inner)]
    return pl.pallas_call(
        functools.partial(_ssd_kernel, reverse=reverse, dcol=dcol, fuse_out=out_stage is not None),
        grid=(batch, nchunks),
        in_specs=in_specs,
        out_specs=[wide(0), state_spec],
        out_shape=[jax.ShapeDtypeStruct((batch * seq_len, d_inner), F32 if out_stage is None else BF16),
                   jax.ShapeDtypeStruct(h0.shape, F32)],
        scratch_shapes=[pltpu.VMEM((SSM_GROUPS, SSM_STATE, gw), F32)],
        compiler_params=_cparams(("parallel", "arbitrary"), 40 << 20),
        name="ssd_scan_bwd" if reverse else "ssd_scan_fwd",
    )(*args)


def _head_norm(t, g_row):
    lane = lax.broadcasted_iota(jnp.int32, t.shape, 1)
    lo = lane < ATTN_QK_DIM
    sq = t * t
    s_lo = jnp.sum(jnp.where(lo, sq, 0.0), axis=-1, keepdims=True)
    s_hi = jnp.sum(jnp.where(lo, 0.0, sq), axis=-1, keepdims=True)
    ms = jnp.where(lo, s_lo, s_hi) * (1.0 / ATTN_QK_DIM)
    return t * lax.rsqrt(ms + EPS) * g_row


def _rope(t, cos, sin_signed):
    lane = lax.broadcasted_iota(jnp.int32, t.shape, 1)
    swapped = jnp.where(lane % 2 == 0, pltpu.roll(t, LANES - 1, 1), pltpu.roll(t, 1, 1))
    return t * cos + swapped * sin_signed


def _qkv_prep_kernel(*refs, use_rope):
    if use_rope:
        q_ref, k_ref, v_ref, gq_ref, gk_ref, cos_ref, sin_ref, qo_ref, ko_ref, vo_ref = refs
    else:
        q_ref, k_ref, v_ref, gq_ref, gk_ref, qo_ref, ko_ref, vo_ref = refs
    vo_ref[...] = v_ref[...].astype(vo_ref.dtype)
    cols = [slice(h * LANES, (h + 1) * LANES) for h in range(ATTN_HEADS)]
    qs = [_head_norm(q_ref[:, c], gq_ref[...]) for c in cols]
    ks = [_head_norm(k_ref[:, c], gk_ref[...]) for c in cols]
    if use_rope:
        qs = [_rope(t, cos_ref[...], sin_ref[...]) for t in qs]
        ks = [_rope(t, cos_ref[...], sin_ref[...]) for t in ks]
    for c, qh, kh in zip(cols, qs, ks):
        qo_ref[:, c] = (qh * Q_SCALE).astype(qo_ref.dtype)
        ko_ref[c, :] = kh.T.astype(ko_ref.dtype)


def qkv_prep(proj, q_blk, width, seq_len, gq, gk, rope):
    m = proj.shape[0]
    tr = _tile(seq_len, 256, LANES)
    tiles_per_seq = seq_len // tr
    blk = lambda c: pl.BlockSpec((tr, width), lambda i: (i, c))
    row = pl.BlockSpec((1, LANES), lambda i: (0, 0))
    in_specs = [blk(q_blk), blk(q_blk + 1), blk(q_blk + 2), row, row]
    args = [proj, proj, proj, jnp.tile(gq, 2).reshape(1, LANES), jnp.tile(gk, 2).reshape(1, LANES)]
    if rope is not None:
        tab = pl.BlockSpec((tr, LANES), lambda i: (i % tiles_per_seq, 0))
        in_specs += [tab, tab]
        args += list(rope)
    out = jax.ShapeDtypeStruct((m, width), BF16)
    kt_spec = pl.BlockSpec((None, width, tr), lambda i: (i // tiles_per_seq, 0, i % tiles_per_seq))
    return pl.pallas_call(
        functools.partial(_qkv_prep_kernel, use_rope=rope is not None),
        grid=(m // tr,),
        in_specs=in_specs,
        out_specs=[blk(0), kt_spec, blk(0)],
        out_shape=[out, jax.ShapeDtypeStruct((m // seq_len, width, seq_len), BF16), out],
        compiler_params=_cparams(("parallel",), 16 * tr * width * 4),
        name="qkv_prep",
    )(*args)


def _attn_kernel(*refs, n_seg, seg_lens, tq, n_blk, key_blk, lam_init):
    q_ref = refs[0]
    kv_refs = refs[1:1 + 2 * n_seg]
    lq_ref, lk_ref, g_ref, o_ref, s_a, s_b, p_a, p_b = refs[1 + 2 * n_seg:]
    lam = (jnp.exp(jnp.sum(lq_ref[0:1, :] * lk_ref[0:1, :], keepdims=True))
           - jnp.exp(jnp.sum(lq_ref[1:2, :] * lk_ref[1:2, :], keepdims=True)) + lam_init)
    lane = lax.broadcasted_iota(jnp.int32, (tq, LANES), 1)
    blocks = []
    off = 0
    for i in range(n_seg):
        blocks += [(i, c0, off + c0) for c0 in range(0, seg_lens[i], key_blk)]
        off += seg_lens[i]
    lane_blocks = range(0, key_blk, LANES)
    n_keys = sum(seg_lens)
    sm_rows = min(tq, 64)

    def score_items(q, s_buf):
        items = []
        for m_idx in range(2):
            in_map = (lane < ATTN_QK_DIM) if m_idx == 0 else (lane >= ATTN_QK_DIM)
            qm = jnp.where(in_map, q, jnp.zeros_like(q))

            def item(qm=qm, m_idx=m_idx, i=0, c0=0, g0=0):
                s_buf[m_idx, :, g0:g0 + key_blk] = _dot(qm, kv_refs[2 * i][:, c0:c0 + key_blk])

            items += [functools.partial(item, i=i, c0=c0, g0=g0) for i, c0, g0 in blocks]
        return items

    def interleave(vpu_items, mxu_items):
        done = 0
        for n, f in enumerate(vpu_items):
            f()
            upto = (n + 1) * len(mxu_items) // len(vpu_items)
            for g in mxu_items[done:upto]:
                g()
            done = upto

    def phase(s_buf, p_buf, next_items):
        denoms = {}
        lane_cols = [slice(c, c + LANES) for c in range(0, n_keys, LANES)]

        def softmax_item(m_idx, r0):
            rows = slice(r0, r0 + sm_rows)
            mx = functools.reduce(jnp.maximum, [s_buf[m_idx, rows, cols] for cols in lane_cols])
            mxb = jnp.broadcast_to(jnp.max(mx, axis=-1, keepdims=True), (sm_rows, LANES))
            acc = None
            for cols in lane_cols:
                p = jnp.exp2(s_buf[m_idx, rows, cols] - mxb)
                acc = p if acc is None else acc + p
                p_buf[m_idx, rows, cols] = p.astype(BF16)
            denoms[m_idx, r0] = jnp.sum(acc, axis=-1, keepdims=True)

        row_starts = range(0, tq, sm_rows)
        interleave([functools.partial(softmax_item, m_idx, r0) for r0 in row_starts for m_idx in range(2)],
                   next_items)
        denom = [jnp.concatenate([denoms[m_idx, r0] for r0 in row_starts], axis=0) for m_idx in range(2)]
        r1 = jnp.broadcast_to(1.0 / denom[0], (tq, LANES)).astype(BF16)
        r2 = jnp.broadcast_to(lam / denom[1], (tq, LANES)).astype(BF16)
        o = None
        for i, c0, g0 in blocks:
            ws = [p_buf[0, :, g0 + c:g0 + c + LANES] * r1 - p_buf[1, :, g0 + c:g0 + c + LANES] * r2
                  for c in lane_blocks]
            part = _dot(jnp.concatenate(ws, axis=1), kv_refs[2 * i + 1][c0:c0 + key_blk, :])
            o = part if o is None else o + part
        o = o * lax.rsqrt(jnp.mean(o * o, axis=-1, keepdims=True) + EPS) * g_ref[...]
        return (o * (1.0 - lam_init)).astype(o_ref.dtype)

    bufs = [(s_a, p_a), (s_b, p_b)]
    for f in score_items(q_ref[0:tq, :], s_a):
        f()
    for j in range(n_blk):
        s_cur, p_cur = bufs[j % 2]
        nxt = score_items(q_ref[(j + 1) * tq:(j + 2) * tq, :], bufs[(j + 1) % 2][0]) if j + 1 < n_blk else []
        o_ref[j * tq:(j + 1) * tq, :] = phase(s_cur, p_cur, nxt)


def diff_attention(q, kv_segments, batch, lam_q, lam_k, subln_g, lam_init):
    mq, width = q.shape
    lq = mq // batch
    tq = _tile(lq // 2, 256, 16)
    n_blk = min(lq // tq, 4)
    steps = lq // (n_blk * tq)
    assert steps * n_blk * tq == lq
    seg_lens = tuple(s for _, _, s in kv_segments)
    n_keys = sum(seg_lens)
    key_blk = 256 if all(s % 256 == 0 for s in seg_lens) else LANES
    q_spec = pl.BlockSpec((n_blk * tq, LANES), lambda b, h, i: (b * steps + i, h))
    in_specs = [q_spec]
    args = [q]
    for kt, v, seg_len in kv_segments:
        in_specs += [pl.BlockSpec((None, LANES, seg_len), lambda b, h, i: (b, h, 0)),
                     pl.BlockSpec((seg_len, LANES), lambda b, h, i: (b, h))]
        args += [kt, v]
    small = pl.BlockSpec((2, ATTN_QK_DIM), lambda b, h, i: (0, 0))
    in_specs += [small, small, pl.BlockSpec((1, LANES), lambda b, h, i: (0, 0))]
    args += [lam_q, lam_k, subln_g.reshape(1, LANES)]
    return pl.pallas_call(
        functools.partial(_attn_kernel, n_seg=len(kv_segments), seg_lens=seg_lens, tq=tq,
                          n_blk=n_blk, key_blk=key_blk, lam_init=lam_init),
        grid=(batch, ATTN_HEADS, steps),
        in_specs=in_specs,
        out_specs=q_spec,
        out_shape=jax.ShapeDtypeStruct((mq, width), BF16),
        scratch_shapes=[pltpu.VMEM((2, tq, n_keys), F32), pltpu.VMEM((2, tq, n_keys), F32),
                        pltpu.VMEM((2, tq, n_keys), BF16), pltpu.VMEM((2, tq, n_keys), BF16)],
        compiler_params=_cparams(("parallel", "parallel", "arbitrary"),
                                 8 * n_keys * LANES * 2 + 2 * tq * n_keys * (12 + 12) + (6 << 20)),
        name="diff_attention",
    )(*args)


def _rope_tables(seq_len):
    pos = jnp.arange(seq_len)
    row, col = pos // GRID_W, pos % GRID_W
    n_freq = ATTN_QK_DIM // 4
    inv = ROPE_BASE ** (-jnp.arange(n_freq, dtype=F32) / n_freq)
    ang = jnp.concatenate([row[:, None] * inv, col[:, None] * inv], axis=-1)
    cos = jnp.repeat(jnp.cos(ang), 2, axis=-1)
    sin = jnp.repeat(jnp.sin(ang), 2, axis=-1) * jnp.tile(jnp.array([-1.0, 1.0], F32), ATTN_QK_DIM // 2)
    return jnp.tile(cos, (1, 2)), jnp.tile(sin, (1, 2))


def _pad_lanes(v):
    return jnp.pad(v.reshape(1, -1), ((0, 0), (0, LANES - v.size)))


def kernel(x, c, ctx, c_ctx, ada_down, ada_up, ada_bias, norm1_g, w_in, gm_norm_g, gm_w, gm_b,
           conv_w, conv_b, dt_bias, a_log, d_skip, ssm_norm_g, q_norm_g, k_norm_g, lam_q, lam_k,
           subln_g, w_out, norm2_g, w_ffn_in, w_ffn_out):
    batch, seq, d = x.shape
    lc = ctx.shape[1]
    depth = w_in.shape[0]
    w_gmlp, w_ssm, w_attn = d // 4, d // 2, d // 4
    n_heads = w_ssm // SSM_HEAD_DIM
    gn = SSM_GROUPS * SSM_STATE
    conv_ch = w_ssm + 2 * gn
    assert batch < ADA_ROWS and 2 * n_heads <= LANES
    assert n_heads == SSM_GROUPS * SSM_HEADS_PER_GROUP and w_attn == ATTN_HEADS * LANES
    assert seq % CHUNK == 0 and lc % CHUNK == 0 and seq % GRID_W == 0
    off_dt = 2 * w_gmlp + w_ssm + conv_ch
    off_q = off_dt + 2 * n_heads

    xs = x.reshape(batch * seq, d)
    ss = ctx.reshape(batch * lc, d)
    cc = jnp.zeros((ADA_ROWS, d), F32).at[:batch].set(c).at[batch].set(c_ctx)
    rope = _rope_tables(seq)
    h_zero = jnp.zeros((batch, SSM_GROUPS, SSM_STATE, SSM_HEADS_PER_GROUP * SSM_HEAD_DIM), F32)
    lat = dict(rows_per_batch=seq, fixed_row=None)
    con = dict(rows_per_batch=None, fixed_row=batch)

    w_in_b = w_in.astype(BF16)
    w_qkv = w_in_b[:, :, off_q:]
    w_dt = jnp.pad(w_in_b[:, :, off_dt:off_q], ((0, 0), (0, 0), (0, LANES - 2 * n_heads)))
    w_o = w_out.astype(BF16)
    w_f1 = w_ffn_in.astype(BF16)
    w_f2 = w_ffn_out.astype(BF16)
    gm_w_b = gm_w.astype(BF16)
    z_blk = 2 * w_gmlp // w_ssm
    xbc_blk = (2 * w_gmlp + w_ssm) // conv_ch

    for l in range(depth):
        last = l == depth - 1
        lam_init = 0.8 - 0.6 * math.exp(-0.3 * l)
        gm_b_full = jnp.repeat(gm_b[l].T, w_gmlp // GMLP_GROUPS, axis=1)
        d_row = jnp.repeat(d_skip[l], SSM_HEAD_DIM).reshape(1, w_ssm)
        dtb_row = _pad_lanes(dt_bias[l])
        alog_row = _pad_lanes(a_log[l])

        mods = adaln(cc, ada_down[l], ada_up[l], ada_bias[l])

        def mixers_in(stream, seq_len, sel, use_rope):
            h = modnorm(stream, norm1_g[l], mods, 0, 1, **sel)
            proj = matmul(h, w_in_b, l, F32, n_cols=off_dt)
            pqkv = matmul(h, w_qkv, l, F32)
            dt_raw = matmul(h, w_dt, l, F32)
            xbc = conv_silu(proj, xbc_blk, conv_ch, seq_len, conv_w[l], conv_b[l])
            q, k, v = qkv_prep(pqkv, 0, w_attn, seq_len, q_norm_g[l], k_norm_g[l],
                               rope if use_rope else None)
            return proj, dt_raw, xbc, q, k, v

        def ssm(proj, dt_raw, xbc, seq_len, h0_f, h0_b):
            y_f, h_f = ssd_scan(xbc, dt_raw, dtb_row, alog_row, h0_f, batch, seq_len, w_ssm, False, 0)
            y, h_b = ssd_scan(xbc, dt_raw, dtb_row, alog_row, h0_b, batch, seq_len, w_ssm, True, n_heads,
                              out_stage=(y_f, proj, z_blk, d_row, ssm_norm_g[l]))
            return y, h_f, h_b

        def mixers_out(stream, a, y, o, sel):
            stream = matmul_gated_residual([a, y, o], w_o, l, stream, mods, 2, **sel)
            h = modnorm(stream, norm2_g[l], mods, 3, 4, **sel)
            act = matmul_swiglu(h, w_f1, l)
            for part in range(2):
                stream = matmul_gated_residual([act], w_f2, l, stream, mods, 5, k_part=(part, 2),
                                               tn_pref=512, **sel)
            return stream

        proj_c, dt_c, xbc_c, q_c, k_c, v_c = mixers_in(ss, lc, con, False)
        proj_l, dt_l, xbc_l, q_l, k_l, v_l = mixers_in(xs, seq, lat, True)

        y_c, hf_c, hb_c = ssm(proj_c, dt_c, xbc_c, lc, h_zero, h_zero)
        y_l, _, _ = ssm(proj_l, dt_l, xbc_l, seq, hf_c, hb_c)

        o_l = diff_attention(q_l, [(k_c, v_c, lc), (k_l, v_l, seq)], batch, lam_q[l], lam_k[l],
                             subln_g[l], lam_init)
        a_l = gmlp(proj_l, 0, 1, w_gmlp, gm_norm_g[l], gm_w_b, l, gm_b_full)
        xs = mixers_out(xs, a_l, y_l, o_l, lat)

        if not last:
            o_c = diff_attention(q_c, [(k_c, v_c, lc)], batch, lam_q[l], lam_k[l], subln_g[l], lam_init)
            a_c = gmlp(proj_c, 0, 1, w_gmlp, gm_norm_g[l], gm_w_b, l, gm_b_full)
            ss = mixers_out(ss, a_c, y_c, o_c, con)

    return xs.reshape(batch, seq, d)
```
